```python
import jax, jax.numpy as jnp
from jax import lax
import numpy as np

D_MODEL = 1024
BATCH = 8
SEQ = 4096
DEPTH = 2

CHUNK = 64
N_A = DEPTH // 2
N_B = DEPTH - N_A
CONV_WIDTH = 3
N_HEADS = 16
HEAD_DIM = D_MODEL // N_HEADS
D_FF = -(-8 * D_MODEL // (3 * 256)) * 256
Q_BLOCK = 128
N_MOD = 6
LN_EPS = 1e-5
DEEPNORM_ALPHA = (2.0 * DEPTH) ** 0.25
DEEPNORM_BETA = (8.0 * DEPTH) ** -0.25

kernel_name = "yoco_shortconv_fox_deepnorm_adaln"


def layer_norm(x, g, b):
    xf = x.astype(jnp.float32)
    mu = jnp.mean(xf, axis=-1, keepdims=True)
    var = jnp.mean(jnp.square(xf - mu), axis=-1, keepdims=True)
    y = (xf - mu) * lax.rsqrt(var + LN_EPS) * g + b
    return y.astype(x.dtype)


def modulate(x, shift, scale):
    return x * (1.0 + scale[:, None, :]) + shift[:, None, :]


def short_conv_mixer(h, w_in, conv_w, w_out):
    d = h.shape[-1]
    gb, gc, v = jnp.split(h @ w_in, 3, axis=-1)
    u = gc * v
    conv = lax.conv_general_dilated(
        u, conv_w[:, None, :].astype(u.dtype),
        window_strides=(1,), padding=[(CONV_WIDTH - 1, 0)],
        dimension_numbers=("NWC", "WIO", "NWC"), feature_group_count=d)
    return (gb * conv) @ w_out


def shared_kv(x, c, w_ada_kv, b_ada_kv, w_k, w_v, w_f, b_f):
    bsz, s, _ = x.shape
    kv_shift, kv_scale = jnp.split(jax.nn.silu(c) @ w_ada_kv + b_ada_kv, 2, axis=-1)
    h = modulate(x, kv_shift, kv_scale)
    k = (h @ w_k).reshape(bsz, s, N_HEADS, HEAD_DIM).transpose(0, 2, 1, 3)
    v = (h @ w_v).reshape(bsz, s, N_HEADS, HEAD_DIM).transpose(0, 2, 1, 3)
    log_f = jax.nn.log_sigmoid((h @ w_f + b_f).astype(jnp.float32))
    log_fcum = jnp.cumsum(log_f, axis=1).transpose(0, 2, 1)
    return k, v, log_fcum


def forgetting_attention(h, w_q, k, v, log_fcum, w_o):
    bsz, s, d = h.shape
    q = (h @ w_q).reshape(bsz, s, N_HEADS, HEAD_DIM).transpose(0, 2, 1, 3) * (HEAD_DIM ** -0.5)
    outs = []
    for i in range(s // Q_BLOCK):
        q0 = i * Q_BLOCK
        kend = q0 + Q_BLOCK
        qb = q[:, :, q0:kend]
        kb = k[:, :, :kend]
        vb = v[:, :, :kend]
        logits = jnp.einsum("bhqd,bhkd->bhqk", qb, kb, preferred_element_type=jnp.float32)
        logits = logits + log_fcum[:, :, q0:kend, None] - log_fcum[:, :, None, :kend]
        qpos = q0 + jnp.arange(Q_BLOCK)
        kpos = jnp.arange(kend)
        logits = jnp.where(kpos[None, :] <= qpos[:, None], logits, -jnp.inf)
        p = jax.nn.softmax(logits, axis=-1)
        outs.append(jnp.einsum("bhqk,bhkd->bhqd", p.astype(vb.dtype), vb))
    o = jnp.concatenate(outs, axis=2).transpose(0, 2, 1, 3).reshape(bsz, s, d)
    return o @ w_o


def swiglu(h, w_gate, w_up, w_down):
    return (jax.nn.silu(h @ w_gate) * (h @ w_up)) @ w_down


def setup_inputs(seed: int = 0) -> dict:
    key = jax.random.key(seed)
    ks = jax.random.split(key, 24)
    D, F, H = D_MODEL, D_FF, N_HEADS
    nrm = lambda k, shape, scale: jax.random.normal(k, shape, jnp.float32) * scale
    return {
        "x": nrm(ks[0], (BATCH, SEQ, D), 1.0),
        "c": nrm(ks[1], (BATCH, D), 1.0),
        "w_ada": nrm(ks[2], (DEPTH, D, N_MOD * D), 0.5 * D ** -0.5),
        "b_ada": nrm(ks[3], (DEPTH, N_MOD * D), 0.02),
        "conv_w_in": nrm(ks[4], (N_A, D, 3 * D), D ** -0.5),
        "conv_w": nrm(ks[5], (N_A, CONV_WIDTH, D), CONV_WIDTH ** -0.5),
        "conv_w_out": nrm(ks[6], (N_A, D, D), DEEPNORM_BETA * D ** -0.5),
        "w_ada_kv": nrm(ks[7], (D, 2 * D), 0.5 * D ** -0.5),
        "b_ada_kv": nrm(ks[8], (2 * D,), 0.02),
        "w_k": nrm(ks[9], (D, D), D ** -0.5),
        "w_v": nrm(ks[10], (D, D), DEEPNORM_BETA * D ** -0.5),
        "w_f": nrm(ks[11], (D, H), D ** -0.5),
        "b_f": jax.random.uniform(ks[12], (H,), jnp.float32, 1.0, 5.0),
        "attn_w_q": nrm(ks[13], (N_B, D, D), D ** -0.5),
        "attn_w_o": nrm(ks[14], (N_B, D, D), DEEPNORM_BETA * D ** -0.5),
        "ffn_w_gate": nrm(ks[15], (DEPTH, D, F), D ** -0.5),
        "ffn_w_up": nrm(ks[16], (DEPTH, D, F), D ** -0.5),
        "ffn_w_down": nrm(ks[17], (DEPTH, F, D), DEEPNORM_BETA * F ** -0.5),
        "ln1_g": 1.0 + nrm(ks[18], (DEPTH, D), 0.02),
        "ln1_b": nrm(ks[19], (DEPTH, D), 0.02),
        "ln2_g": 1.0 + nrm(ks[20], (DEPTH, D), 0.02),
        "ln2_b": nrm(ks[21], (DEPTH, D), 0.02),
    }


def reference(x, c, w_ada, b_ada, conv_w_in, conv_w, conv_w_out, w_ada_kv, b_ada_kv,
              w_k, w_v, w_f, b_f, attn_w_q, attn_w_o, ffn_w_gate, ffn_w_up, ffn_w_down,
              ln1_g, ln1_b, ln2_g, ln2_b):
    c_act = jax.nn.silu(c)
    k = v = log_fcum = None
    for l in range(DEPTH):
        sh1, sc1, g1, sh2, sc2, g2 = jnp.split(c_act @ w_ada[l] + b_ada[l], N_MOD, axis=-1)
        h = modulate(x, sh1, sc1)
        if l < N_A:
            y = short_conv_mixer(h, conv_w_in[l], conv_w[l], conv_w_out[l])
        else:
            if l == N_A:
                k, v, log_fcum = shared_kv(x, c, w_ada_kv, b_ada_kv, w_k, w_v, w_f, b_f)
            j = l - N_A
            y = forgetting_attention(h, attn_w_q[j], k, v, log_fcum, attn_w_o[j])
        x = layer_norm(DEEPNORM_ALPHA * x + g1[:, None, :] * y, ln1_g[l], ln1_b[l])
        h = modulate(x, sh2, sc2)
        y = swiglu(h, ffn_w_gate[l], ffn_w_up[l], ffn_w_down[l])
        x = layer_norm(DEEPNORM_ALPHA * x + g2[:, None, :] * y, ln2_g[l], ln2_b[l])
    return x
```

```python
import functools

import jax
import jax.numpy as jnp
from jax import lax
from jax.experimental import pallas as pl
from jax.experimental.pallas import tpu as pltpu

F32 = jnp.float32
BF16 = jnp.bfloat16

LN_EPS = 1e-5
HEAD_DIM = 64
N_MOD = 6
V7X_LANES = 128
V7X_VMEM_BYTES = 64 * 2**20
NEG_BIG = -1e30
HEADS_PER_STEP = 2
SUM_ROWS = 16


def _dot(a, b):
    return jnp.dot(a, b, preferred_element_type=F32)


def _dot_nt(a, b):
    return lax.dot_general(a, b, (((1,), (1,)), ((), ())), preferred_element_type=F32)


def _split3(x):
    hi = x.astype(BF16)
    r = x - hi.astype(F32)
    mid = r.astype(BF16)
    lo = (r - mid.astype(F32)).astype(BF16)
    return hi, mid, lo


def _layer_norm(x, g, b):
    mu = jnp.mean(x, axis=-1, keepdims=True)
    xc = x - mu
    var = jnp.mean(xc * xc, axis=-1, keepdims=True)
    return xc * lax.rsqrt(var + LN_EPS) * g + b


def _silu(x):
    return x / (1.0 + jnp.exp(-x))


def _const_spec(shape):
    nd = len(shape)
    return pl.BlockSpec(shape, lambda *_: (0,) * nd, pipeline_mode=pl.Buffered(1))


def _params(sem, vmem_mb):
    return pltpu.CompilerParams(dimension_semantics=sem, vmem_limit_bytes=vmem_mb * 2**20)


def _mods_kernel(c_ref, wada_ref, wkv_ref, b_ref, o_ref, *, n_ada):
    n = pl.program_id(0)
    ca = _silu(c_ref[...])
    ca_hi = ca.astype(BF16)
    ca_lo = (ca - ca_hi.astype(F32)).astype(BF16)

    def compute(w):
        w_hi = w.astype(BF16)
        w_lo = (w - w_hi.astype(F32)).astype(BF16)
        return _dot(ca_hi, w_hi) + _dot(ca_lo, w_hi) + _dot(ca_hi, w_lo)

    @pl.when(n < n_ada)
    def _():
        o_ref[0] = compute(wada_ref[0]) + b_ref[0]

    @pl.when(n >= n_ada)
    def _():
        o_ref[0] = compute(wkv_ref[...]) + b_ref[0]


def _mods(c, w_ada, b_ada, w_ada_kv, b_ada_kv):
    bsz, d = c.shape
    depth = w_ada.shape[0]
    n_ada = depth * N_MOD
    n_tot = n_ada + 2
    b_all = jnp.concatenate([b_ada.reshape(n_ada, d), b_ada_kv.reshape(2, d)], axis=0).reshape(n_tot, 1, d)
    out = pl.pallas_call(
        functools.partial(_mods_kernel, n_ada=n_ada),
        grid=(n_tot,),
        in_specs=[
            pl.BlockSpec((bsz, d), lambda n: (0, 0)),
            pl.BlockSpec((1, d, d), lambda n: (jnp.minimum(n, n_ada - 1) // N_MOD, 0, jnp.minimum(n, n_ada - 1) % N_MOD)),
            pl.BlockSpec((d, d), lambda n: (0, jnp.maximum(n - n_ada, 0))),
            pl.BlockSpec((1, 1, d), lambda n: (n, 0, 0)),
        ],
        out_specs=pl.BlockSpec((1, bsz, d), lambda n: (n, 0, 0)),
        out_shape=jax.ShapeDtypeStruct((n_tot, bsz, d), F32),
        compiler_params=_params(("arbitrary",), 40),
        name="mods",
    )(c, w_ada, w_ada_kv, b_all)
    mods = jnp.transpose(out, (1, 0, 2))
    per_layer = [mods[:, l * N_MOD:(l + 1) * N_MOD] for l in range(depth)]
    return per_layer, mods[:, n_ada:]


def _conv_kernel(x_ref, mod_ref, win_ref, cw_ref, z_ref, carry_ref, *, tm, d):
    @pl.when(pl.program_id(1) == 0)
    def _():
        carry_ref[...] = jnp.zeros_like(carry_ref)

    mod = mod_ref[0]
    h = (x_ref[0] * (1.0 + mod[1:2]) + mod[0:1]).astype(BF16)
    gc = _dot(h, win_ref[:, d:2 * d])
    v = _dot(h, win_ref[:, 2 * d:3 * d])
    u = gc * v
    row = lax.broadcasted_iota(jnp.int32, (tm, 1), 0)
    c_last = carry_ref[7:8, :]
    c_prev = carry_ref[6:7, :]
    um1 = jnp.where(row == 0, c_last, pltpu.roll(u, 1, axis=0))
    um2 = jnp.where(row == 0, c_prev, jnp.where(row == 1, c_last, pltpu.roll(u, 2, axis=0)))
    cw = cw_ref[...]
    conv = cw[0:1] * um2 + cw[1:2] * um1 + cw[2:3] * u
    carry_ref[...] = u[tm - 8:tm, :]
    gb = _dot(h, win_ref[:, 0:d])
    z_ref[0] = (gb * conv).astype(BF16)


def _conv_front(x, mod, w_in, conv_w, *, tm):
    bsz, s, d = x.shape
    return pl.pallas_call(
        functools.partial(_conv_kernel, tm=tm, d=d),
        grid=(bsz, s // tm),
        in_specs=[
            pl.BlockSpec((1, tm, d), lambda b, t: (b, t, 0)),
            pl.BlockSpec((1, N_MOD, d), lambda b, t: (b, 0, 0)),
            _const_spec(w_in.shape),
            _const_spec(conv_w.shape),
        ],
        out_specs=pl.BlockSpec((1, tm, d), lambda b, t: (b, t, 0)),
        out_shape=jax.ShapeDtypeStruct((bsz, s, d), BF16),
        scratch_shapes=[pltpu.VMEM((8, d), F32)],
        compiler_params=_params(("parallel", "arbitrary"), 48),
        name="conv_front",
    )(x, mod, w_in, conv_w)


def _tail_kernel(x_ref, z_ref, mod_ref, wp_ref, wg_ref, wu_ref, wd_ref, ln_ref, o_ref, *, alpha):
    mod = mod_ref[0]
    ln = ln_ref[...]
    y = _dot(z_ref[0], wp_ref[...])
    x1 = _layer_norm(alpha * x_ref[0] + mod[2:3] * y, ln[0:1], ln[1:2])
    h = (x1 * (1.0 + mod[4:5]) + mod[3:4]).astype(BF16)
    g = _dot(h, wg_ref[...])
    u = _dot(h, wu_ref[...])
    a = (_silu(g) * u).astype(BF16)
    y2 = _dot(a, wd_ref[...])
    o_ref[0] = _layer_norm(alpha * x1 + mod[5:6] * y2, ln[2:3], ln[3:4])


def _tail(x, z, mod, w_proj, w_gate, w_up, w_down, ln, *, alpha, tm):
    bsz, s, d = x.shape
    return pl.pallas_call(
        functools.partial(_tail_kernel, alpha=alpha),
        grid=(bsz, s // tm),
        in_specs=[
            pl.BlockSpec((1, tm, d), lambda b, t: (b, t, 0)),
            pl.BlockSpec((1, tm, d), lambda b, t: (b, t, 0)),
            pl.BlockSpec((1, N_MOD, d), lambda b, t: (b, 0, 0)),
            _const_spec(w_proj.shape),
            _const_spec(w_gate.shape),
            _const_spec(w_up.shape),
            _const_spec(w_down.shape),
            _const_spec(ln.shape),
        ],
        out_specs=pl.BlockSpec((1, tm, d), lambda b, t: (b, t, 0)),
        out_shape=jax.ShapeDtypeStruct((bsz, s, d), F32),
        compiler_params=_params(("parallel", "parallel"), 56),
        name="tail",
    )(x, z, mod, w_proj, w_gate, w_up, w_down, ln)


def _qkv_kernel(x_ref, mod_ref, modkv_ref, wk_ref, wvt_ref, wqt_ref, wf_ref, bf_ref,
                qt_ref, k_ref, vt_ref, aug_ref, carry_ref, *, tm, n_heads, q_scale):
    @pl.when(pl.program_id(1) == 0)
    def _():
        carry_ref[...] = jnp.zeros_like(carry_ref)

    x = x_ref[0]
    mod = mod_ref[0]
    modkv = modkv_ref[0]
    hkv = (x * (1.0 + modkv[1:2]) + modkv[0:1]).astype(BF16)
    hq = (x * (1.0 + mod[1:2]) + mod[0:1]).astype(BF16)
    k_ref[0] = _dot(hkv, wk_ref[...]).astype(BF16)
    vt_ref[0, 0] = _dot_nt(wvt_ref[...], hkv).astype(BF16)
    qt_ref[0] = (_dot_nt(wqt_ref[...], hq) * q_scale).astype(BF16)

    zf = _dot(hkv, wf_ref[...]) + bf_ref[...]
    lf = jnp.minimum(zf, 0.0) - jnp.log1p(jnp.exp(-jnp.abs(zf)))
    lane = lax.broadcasted_iota(jnp.int32, lf.shape, 1)
    lf = jnp.where(lane < n_heads, lf, 0.0)
    tri = jnp.where(lax.broadcasted_iota(jnp.int32, (tm, tm), 0) >= lax.broadcasted_iota(jnp.int32, (tm, tm), 1),
                    1.0, 0.0).astype(BF16)
    hi, mid, lo = _split3(lf)
    fcum = _dot(tri, hi) + _dot(tri, mid) + _dot(tri, lo) + carry_ref[0:1, :]
    carry_ref[0:1, :] = fcum[tm - 1:tm, :]
    a_hi, a_mid, a_lo = _split3(-fcum)
    aug = (a_hi.astype(F32) + pltpu.roll(a_mid.astype(F32), n_heads, axis=1)
           + pltpu.roll(a_lo.astype(F32), 2 * n_heads, axis=1))
    aug_ref[0] = aug.astype(BF16)


def _qkv(x, mod, modkv, w_k, w_vt, w_qt, w_f, b_f, *, tm, n_heads):
    bsz, s, d = x.shape
    nt = s // tm
    lanes = w_f.shape[1]
    return pl.pallas_call(
        functools.partial(_qkv_kernel, tm=tm, n_heads=n_heads, q_scale=HEAD_DIM ** -0.5),
        grid=(bsz, nt),
        in_specs=[
            pl.BlockSpec((1, tm, d), lambda b, t: (b, t, 0)),
            pl.BlockSpec((1, N_MOD, d), lambda b, t: (b, 0, 0)),
            pl.BlockSpec((1, 2, d), lambda b, t: (b, 0, 0)),
            _const_spec(w_k.shape),
            _const_spec(w_vt.shape),
            _const_spec(w_qt.shape),
            _const_spec(w_f.shape),
            _const_spec(b_f.shape),
        ],
        out_specs=[
            pl.BlockSpec((1, d, tm), lambda b, t: (b, 0, t)),
            pl.BlockSpec((1, tm, d), lambda b, t: (b, t, 0)),
            pl.BlockSpec((1, 1, d, tm), lambda b, t: (b, t, 0, 0)),
            pl.BlockSpec((1, tm, lanes), lambda b, t: (b, t, 0)),
        ],
        out_shape=[
            jax.ShapeDtypeStruct((bsz, d, s), BF16),
            jax.ShapeDtypeStruct((bsz, s, d), BF16),
            jax.ShapeDtypeStruct((bsz, nt, d, tm), BF16),
            jax.ShapeDtypeStruct((bsz, s, lanes), BF16),
        ],
        scratch_shapes=[pltpu.VMEM((8, lanes), F32)],
        compiler_params=_params(("parallel", "arbitrary"), 48),
        name="qkv",
    )(x, mod, modkv, w_k, w_vt, w_qt, w_f, b_f)


def _attn_kernel(qt_ref, k_ref, aug_ref, vt_ref, o_ref, rhs_ref, m_ref, acc_ref, *, tq, tk, n_heads):
    p = pl.program_id(1)
    i = pl.program_id(2)
    hd = HEAD_DIM
    blk = HEADS_PER_STEP * hd

    qt = qt_ref[0].astype(F32)
    r = lax.broadcasted_iota(jnp.int32, (blk, tq), 0)
    for hh in range(HEADS_PER_STEP):
        head = HEADS_PER_STEP * p + hh
        sel = (r == head) | (r == head + n_heads) | (r == head + 2 * n_heads)
        ind = jnp.where(sel, 1.0, 0.0).astype(BF16)
        qm = jnp.where((r >= hh * hd) & (r < (hh + 1) * hd), qt, 0.0).astype(BF16)
        rhs_ref[0:blk, hh * tq:(hh + 1) * tq] = qm
        rhs_ref[blk:2 * blk, hh * tq:(hh + 1) * tq] = ind

    m_ref[...] = jnp.full_like(m_ref, NEG_BIG)
    acc_ref[...] = jnp.zeros_like(acc_ref)
    ones = jnp.ones((SUM_ROWS, tk), BF16)

    def tile(j, masked):
        ks = pl.multiple_of(j * tk, tk)
        kx = jnp.concatenate([k_ref[0, pl.ds(ks, tk), :], aug_ref[0, pl.ds(ks, tk), :]], axis=1)
        s = _dot(kx, rhs_ref[...])
        if masked:
            krow = lax.broadcasted_iota(jnp.int32, s.shape, 0)
            qcol = lax.broadcasted_iota(jnp.int32, s.shape, 1)
            qcol = jnp.where(qcol >= tq, qcol - tq, qcol)
            s = jnp.where(krow <= qcol, s, NEG_BIG)
        m_old = m_ref[...]
        m_new = jnp.maximum(m_old, jnp.max(s, axis=0, keepdims=True))
        alpha = jnp.exp(m_old - m_new)
        pr = jnp.exp(s - m_new).astype(BF16)
        m_ref[...] = m_new
        for hh in range(HEADS_PER_STEP):
            vx = jnp.concatenate([vt_ref[0, j, hh * hd:(hh + 1) * hd, :], ones], axis=0)
            pv = _dot(vx, pr[:, hh * tq:(hh + 1) * tq])
            acc_ref[hh] = alpha[:, hh * tq:(hh + 1) * tq] * acc_ref[hh] + pv

    def body(j, carry):
        tile(j, False)
        return carry

    lax.fori_loop(0, i, body, 0)
    tile(i, True)

    outs = []
    for hh in range(HEADS_PER_STEP):
        acc = acc_ref[hh]
        outs.append(acc[0:hd] / acc[hd:hd + 1])
    o_ref[0] = jnp.concatenate(outs, axis=0).T.astype(BF16)


def _attention(qt, k, aug, vt, *, tq, n_heads):
    bsz, d, s = qt.shape
    nk, tk = vt.shape[1], vt.shape[3]
    assert tq == tk, "the causal mask is applied to the diagonal tile only"
    blk = HEADS_PER_STEP * HEAD_DIM
    lanes = aug.shape[2]
    return pl.pallas_call(
        functools.partial(_attn_kernel, tq=tq, tk=tk, n_heads=n_heads),
        grid=(bsz, d // blk, s // tq),
        in_specs=[
            pl.BlockSpec((1, blk, tq), lambda b, p, i: (b, p, i)),
            pl.BlockSpec((1, s, blk), lambda b, p, i: (b, 0, p)),
            pl.BlockSpec((1, s, lanes), lambda b, p, i: (b, 0, 0)),
            pl.BlockSpec((1, nk, blk, tk), lambda b, p, i: (b, 0, p, 0)),
        ],
        out_specs=pl.BlockSpec((1, tq, blk), lambda b, p, i: (b, i, p)),
        out_shape=jax.ShapeDtypeStruct((bsz, s, d), BF16),
        scratch_shapes=[
            pltpu.VMEM((blk + lanes, HEADS_PER_STEP * tq), BF16),
            pltpu.VMEM((1, HEADS_PER_STEP * tq), F32),
            pltpu.VMEM((HEADS_PER_STEP, HEAD_DIM + SUM_ROWS, tq), F32),
        ],
        compiler_params=_params(("parallel", "parallel", "arbitrary"), 32),
        name="attention",
    )(qt, k, aug, vt)


def kernel(x, c, w_ada, b_ada, conv_w_in, conv_w, conv_w_out, w_ada_kv, b_ada_kv, w_k, w_v, w_f, b_f,
           attn_w_q, attn_w_o, ffn_w_gate, ffn_w_up, ffn_w_down, ln1_g, ln1_b, ln2_g, ln2_b):
    depth = w_ada.shape[0]
    n_a = conv_w_in.shape[0]
    d = x.shape[-1]
    n_heads = w_f.shape[1]
    assert d // n_heads == HEAD_DIM and 3 * n_heads <= V7X_LANES
    alpha = (2.0 * depth) ** 0.25
    tm = 256

    mods, modkv = _mods(c, w_ada, b_ada, w_ada_kv, b_ada_kv)
    bf = lambda w: w.astype(BF16)
    w_f_pad = jnp.pad(w_f, ((0, 0), (0, V7X_LANES - n_heads))).astype(BF16)
    b_f_pad = jnp.pad(b_f, (0, V7X_LANES - n_heads)).reshape(1, V7X_LANES)

    kv = None
    for l in range(depth):
        ln = jnp.stack([ln1_g[l], ln1_b[l], ln2_g[l], ln2_b[l]])
        if l < n_a:
            z = _conv_front(x, mods[l], bf(conv_w_in[l]), conv_w[l], tm=tm)
            w_proj = conv_w_out[l]
        else:
            j = l - n_a
            if kv is None:
                qt, k, vt, aug = _qkv(x, mods[l], modkv, bf(w_k), bf(w_v.T), bf(attn_w_q[j].T), w_f_pad, b_f_pad,
                                      tm=tm, n_heads=n_heads)
                kv = (k, aug, vt)
            else:
                raise NotImplementedError("more than one attention layer needs a q-only projection kernel")
            z = _attention(qt, *kv, tq=tm, n_heads=n_heads)
            w_proj = attn_w_o[j]
        x = _tail(x, z, mods[l], bf(w_proj), bf(ffn_w_gate[l]), bf(ffn_w_up[l]), bf(ffn_w_down[l]), ln,
                  alpha=alpha, tm=tm)
    return x
```

```python
import functools

import jax
import jax.numpy as jnp
from jax import lax
from jax.experimental import pallas as pl
from jax.experimental.pallas import tpu as pltpu

F32 = jnp.float32
BF16 = jnp.bfloat16

LN_EPS = 1e-5
HEAD_DIM = 64
N_MOD = 6
V7X_LANES = 128
V7X_VMEM_BYTES = 64 * 2**20
NEG_BIG = -1e30
LOG2_E = 1.4426950408889634
HEADS_PER_STEP = 2
SUM_ROWS = 16


def _dot(a, b):
    return jnp.dot(a, b, preferred_element_type=F32)


def _dot_nt(a, b):
    return lax.dot_general(a, b, (((1,), (1,)), ((), ())), preferred_element_type=F32)


def _split3(x):
    hi = x.astype(BF16)
    r = x - hi.astype(F32)
    mid = r.astype(BF16)
    lo = (r - mid.astype(F32)).astype(BF16)
    return hi, mid, lo


def _layer_norm(x, g, b):
    mu = jnp.mean(x, axis=-1, keepdims=True)
    xc = x - mu
    var = jnp.mean(xc * xc, axis=-1, keepdims=True)
    return xc * lax.rsqrt(var + LN_EPS) * g + b


def _silu(x):
    return x / (1.0 + jnp.exp(-x))


def _const_spec(shape):
    nd = len(shape)
    return pl.BlockSpec(shape, lambda *_: (0,) * nd, pipeline_mode=pl.Buffered(1))


def _params(sem, vmem_mb):
    return pltpu.CompilerParams(dimension_semantics=sem, vmem_limit_bytes=vmem_mb * 2**20)


def _mods_kernel(c_ref, wada_ref, wkv_ref, b_ref, o_ref, *, n_ada):
    n = pl.program_id(0)
    ca = _silu(c_ref[...])
    ca_hi = ca.astype(BF16)
    ca_lo = (ca - ca_hi.astype(F32)).astype(BF16)

    def compute(w):
        w_hi = w.astype(BF16)
        w_lo = (w - w_hi.astype(F32)).astype(BF16)
        return _dot(ca_hi, w_hi) + _dot(ca_lo, w_hi) + _dot(ca_hi, w_lo)

    @pl.when(n < n_ada)
    def _():
        o_ref[0] = compute(wada_ref[0]) + b_ref[0]

    @pl.when(n >= n_ada)
    def _():
        o_ref[0] = compute(wkv_ref[...]) + b_ref[0]


def _mods(c, w_ada, b_ada, w_ada_kv, b_ada_kv):
    bsz, d = c.shape
    depth = w_ada.shape[0]
    n_ada = depth * N_MOD
    n_tot = n_ada + 2
    b_all = jnp.concatenate([b_ada.reshape(n_ada, d), b_ada_kv.reshape(2, d)], axis=0).reshape(n_tot, 1, d)
    out = pl.pallas_call(
        functools.partial(_mods_kernel, n_ada=n_ada),
        grid=(n_tot,),
        in_specs=[
            pl.BlockSpec((bsz, d), lambda n: (0, 0)),
            pl.BlockSpec((1, d, d), lambda n: (jnp.minimum(n, n_ada - 1) // N_MOD, 0, jnp.minimum(n, n_ada - 1) % N_MOD)),
            pl.BlockSpec((d, d), lambda n: (0, jnp.maximum(n - n_ada, 0))),
            pl.BlockSpec((1, 1, d), lambda n: (n, 0, 0)),
        ],
        out_specs=pl.BlockSpec((1, bsz, d), lambda n: (n, 0, 0)),
        out_shape=jax.ShapeDtypeStruct((n_tot, bsz, d), F32),
        compiler_params=_params(("arbitrary",), 40),
        name="mods",
    )(c, w_ada, w_ada_kv, b_all)
    mods = jnp.transpose(out, (1, 0, 2))
    per_layer = [mods[:, l * N_MOD:(l + 1) * N_MOD] for l in range(depth)]
    return per_layer, mods[:, n_ada:]


def _conv_kernel(x_ref, mod_ref, win_ref, cw_ref, z_ref, carry_ref, *, tm, d):
    @pl.when(pl.program_id(1) == 0)
    def _():
        carry_ref[...] = jnp.zeros_like(carry_ref)

    mod = mod_ref[0]
    h = (x_ref[0] * (1.0 + mod[1:2]) + mod[0:1]).astype(BF16)
    gc = _dot(h, win_ref[:, d:2 * d])
    v = _dot(h, win_ref[:, 2 * d:3 * d])
    u = gc * v
    row = lax.broadcasted_iota(jnp.int32, (tm, 1), 0)
    c_last = carry_ref[7:8, :]
    c_prev = carry_ref[6:7, :]
    um1 = jnp.where(row == 0, c_last, pltpu.roll(u, 1, axis=0))
    um2 = jnp.where(row == 0, c_prev, jnp.where(row == 1, c_last, pltpu.roll(u, 2, axis=0)))
    cw = cw_ref[...]
    conv = cw[0:1] * um2 + cw[1:2] * um1 + cw[2:3] * u
    carry_ref[...] = u[tm - 8:tm, :]
    gb = _dot(h, win_ref[:, 0:d])
    z_ref[0] = (gb * conv).astype(BF16)


def _conv_front(x, mod, w_in, conv_w, *, tm):
    bsz, s, d = x.shape
    return pl.pallas_call(
        functools.partial(_conv_kernel, tm=tm, d=d),
        grid=(bsz, s // tm),
        in_specs=[
            pl.BlockSpec((1, tm, d), lambda b, t: (b, t, 0)),
            pl.BlockSpec((1, N_MOD, d), lambda b, t: (b, 0, 0)),
            _const_spec(w_in.shape),
            _const_spec(conv_w.shape),
        ],
        out_specs=pl.BlockSpec((1, tm, d), lambda b, t: (b, t, 0)),
        out_shape=jax.ShapeDtypeStruct((bsz, s, d), BF16),
        scratch_shapes=[pltpu.VMEM((8, d), F32)],
        compiler_params=_params(("parallel", "arbitrary"), 48),
        name="conv_front",
    )(x, mod, w_in, conv_w)


def _tail_kernel(x_ref, z_ref, mod_ref, wp_ref, wg_ref, wu_ref, wd_ref, ln_ref, o_ref, *, alpha):
    mod = mod_ref[0]
    ln = ln_ref[...]
    y = _dot(z_ref[0], wp_ref[...])
    x1 = _layer_norm(alpha * x_ref[0] + mod[2:3] * y, ln[0:1], ln[1:2])
    h = (x1 * (1.0 + mod[4:5]) + mod[3:4]).astype(BF16)
    g = _dot(h, wg_ref[...])
    u = _dot(h, wu_ref[...])
    a = (_silu(g) * u).astype(BF16)
    y2 = _dot(a, wd_ref[...])
    o_ref[0] = _layer_norm(alpha * x1 + mod[5:6] * y2, ln[2:3], ln[3:4])


def _tail(x, z, mod, w_proj, w_gate, w_up, w_down, ln, *, alpha, tm):
    bsz, s, d = x.shape
    return pl.pallas_call(
        functools.partial(_tail_kernel, alpha=alpha),
        grid=(bsz, s // tm),
        in_specs=[
            pl.BlockSpec((1, tm, d), lambda b, t: (b, t, 0)),
            pl.BlockSpec((1, tm, d), lambda b, t: (b, t, 0)),
            pl.BlockSpec((1, N_MOD, d), lambda b, t: (b, 0, 0)),
            _const_spec(w_proj.shape),
            _const_spec(w_gate.shape),
            _const_spec(w_up.shape),
            _const_spec(w_down.shape),
            _const_spec(ln.shape),
        ],
        out_specs=pl.BlockSpec((1, tm, d), lambda b, t: (b, t, 0)),
        out_shape=jax.ShapeDtypeStruct((bsz, s, d), F32),
        compiler_params=_params(("parallel", "parallel"), 56),
        name="tail",
    )(x, z, mod, w_proj, w_gate, w_up, w_down, ln)


def _qkv_kernel(x_ref, mod_ref, modkv_ref, wk_ref, wvt_ref, wqt_ref, wf_ref, bf_ref,
                qt_ref, k_ref, vt_ref, aug_ref, carry_ref, *, tm, n_heads, q_scale):
    @pl.when(pl.program_id(1) == 0)
    def _():
        carry_ref[...] = jnp.zeros_like(carry_ref)

    x = x_ref[0]
    mod = mod_ref[0]
    modkv = modkv_ref[0]
    hkv = (x * (1.0 + modkv[1:2]) + modkv[0:1]).astype(BF16)
    hq = (x * (1.0 + mod[1:2]) + mod[0:1]).astype(BF16)
    k_ref[0] = _dot(hkv, wk_ref[...]).astype(BF16)
    vt_ref[0, 0] = _dot_nt(wvt_ref[...], hkv).astype(BF16)
    qt_ref[0] = (_dot_nt(wqt_ref[...], hq) * q_scale).astype(BF16)

    zf = _dot(hkv, wf_ref[...]) + bf_ref[...]
    lf = jnp.minimum(zf, 0.0) - jnp.log1p(jnp.exp(-jnp.abs(zf)))
    lane = lax.broadcasted_iota(jnp.int32, lf.shape, 1)
    lf = jnp.where(lane < n_heads, lf, 0.0)
    tri = jnp.where(lax.broadcasted_iota(jnp.int32, (tm, tm), 0) >= lax.broadcasted_iota(jnp.int32, (tm, tm), 1),
                    1.0, 0.0).astype(BF16)
    hi, mid, lo = _split3(lf)
    fcum = _dot(tri, hi) + _dot(tri, mid) + _dot(tri, lo) + carry_ref[0:1, :]
    carry_ref[0:1, :] = fcum[tm - 1:tm, :]
    a_hi, a_mid, a_lo = _split3(-LOG2_E * fcum)
    aug = (a_hi.astype(F32) + pltpu.roll(a_mid.astype(F32), n_heads, axis=1)
           + pltpu.roll(a_lo.astype(F32), 2 * n_heads, axis=1))
    aug_ref[0] = aug.astype(BF16)


def _qkv(x, mod, modkv, w_k, w_vt, w_qt, w_f, b_f, *, tm, n_heads):
    bsz, s, d = x.shape
    nt = s // tm
    lanes = w_f.shape[1]
    return pl.pallas_call(
        functools.partial(_qkv_kernel, tm=tm, n_heads=n_heads, q_scale=LOG2_E * HEAD_DIM ** -0.5),
        grid=(bsz, nt),
        in_specs=[
            pl.BlockSpec((1, tm, d), lambda b, t: (b, t, 0)),
            pl.BlockSpec((1, N_MOD, d), lambda b, t: (b, 0, 0)),
            pl.BlockSpec((1, 2, d), lambda b, t: (b, 0, 0)),
            _const_spec(w_k.shape),
            _const_spec(w_vt.shape),
            _const_spec(w_qt.shape),
            _const_spec(w_f.shape),
            _const_spec(b_f.shape),
        ],
        out_specs=[
            pl.BlockSpec((1, d, tm), lambda b, t: (b, 0, t)),
            pl.BlockSpec((1, tm, d), lambda b, t: (b, t, 0)),
            pl.BlockSpec((1, 1, d, tm), lambda b, t: (b, t, 0, 0)),
            pl.BlockSpec((1, tm, lanes), lambda b, t: (b, t, 0)),
        ],
        out_shape=[
            jax.ShapeDtypeStruct((bsz, d, s), BF16),
            jax.ShapeDtypeStruct((bsz, s, d), BF16),
            jax.ShapeDtypeStruct((bsz, nt, d, tm), BF16),
            jax.ShapeDtypeStruct((bsz, s, lanes), BF16),
        ],
        scratch_shapes=[pltpu.VMEM((8, lanes), F32)],
        compiler_params=_params(("parallel", "arbitrary"), 48),
        name="qkv",
    )(x, mod, modkv, w_k, w_vt, w_qt, w_f, b_f)


def _attn_kernel(qt_ref, k_ref, aug_ref, vt_ref, o_ref, rhs_ref, s_ref, mt_ref, m_ref, acc_ref,
                 *, tq, tk, n_heads, n_streams):
    g = pl.program_id(1)
    i = pl.program_id(2)
    hd = HEAD_DIM
    blk = HEADS_PER_STEP * hd

    r = lax.broadcasted_iota(jnp.int32, (blk, tq), 0)
    for u in range(n_streams):
        qt = qt_ref[0, u * blk:(u + 1) * blk, :].astype(F32)
        for hh in range(HEADS_PER_STEP):
            head = (g * n_streams + u) * HEADS_PER_STEP + hh
            sel = (r == head) | (r == head + n_heads) | (r == head + 2 * n_heads)
            rhs_ref[u, 0:blk, hh * tq:(hh + 1) * tq] = jnp.where(
                (r >= hh * hd) & (r < (hh + 1) * hd), qt, 0.0).astype(BF16)
            rhs_ref[u, blk:2 * blk, hh * tq:(hh + 1) * tq] = jnp.where(sel, 1.0, 0.0).astype(BF16)

    m_ref[...] = jnp.full_like(m_ref, NEG_BIG)
    acc_ref[...] = jnp.zeros_like(acc_ref)
    ones = jnp.ones((SUM_ROWS, tk), BF16)

    def qk_stage(u, j, buf, masked=False):
        ks = pl.multiple_of(j * tk, tk)
        kx = jnp.concatenate([k_ref[0, pl.ds(ks, tk), u * blk:(u + 1) * blk], aug_ref[0, pl.ds(ks, tk), :]], axis=1)
        s = _dot(kx, rhs_ref[u])
        if masked:
            krow = lax.broadcasted_iota(jnp.int32, s.shape, 0)
            qcol = lax.broadcasted_iota(jnp.int32, s.shape, 1)
            qcol = jnp.where(qcol >= tq, qcol - tq, qcol)
            s = jnp.where(krow <= qcol, s, NEG_BIG)
        s_ref[buf, u] = s
        mt_ref[buf, u] = jnp.max(s, axis=0, keepdims=True)

    def pv_stage(u, j, buf):
        m_old = m_ref[u]
        m_new = jnp.maximum(m_old, mt_ref[buf, u])
        alpha = jnp.exp2(m_old - m_new)
        pr = jnp.exp2(s_ref[buf, u] - m_new).astype(BF16)
        m_ref[u] = m_new
        for hh in range(HEADS_PER_STEP):
            row0 = u * blk + hh * hd
            vx = jnp.concatenate([vt_ref[0, j, row0:row0 + hd, :], ones], axis=0)
            pv = _dot(vx, pr[:, hh * tq:(hh + 1) * tq])
            acc_ref[u, hh] = alpha[:, hh * tq:(hh + 1) * tq] * acc_ref[u, hh] + pv

    def all_streams(fn, *args, **kwargs):
        for u in range(n_streams):
            fn(u, *args, **kwargs)

    odd = i % 2 == 1

    @pl.when(i == 0)
    def _():
        all_streams(qk_stage, 0, 0, masked=True)

    @pl.when(odd)
    def _():
        all_streams(qk_stage, 0, 1)

    @pl.when((i > 0) & jnp.logical_not(odd))
    def _():
        all_streams(qk_stage, 0, 0)
        all_streams(pv_stage, 0, 0)
        all_streams(qk_stage, 1, 1)

    t_start = 1 - i % 2

    def body(jj, carry):
        t = t_start + 2 * jj
        all_streams(qk_stage, t + 1, 0)
        all_streams(pv_stage, t, 1)
        all_streams(qk_stage, t + 2, 1)
        all_streams(pv_stage, t + 1, 0)
        return carry

    lax.fori_loop(0, jnp.maximum(i - 1, 0) // 2, body, 0)

    @pl.when(i > 0)
    def _():
        all_streams(pv_stage, i - 1, 1)
        all_streams(qk_stage, i, 0, masked=True)

    all_streams(pv_stage, i, 0)

    outs = []
    for u in range(n_streams):
        for hh in range(HEADS_PER_STEP):
            acc = acc_ref[u, hh]
            outs.append(acc[0:hd] / acc[hd:hd + 1])
    o_ref[0] = jnp.concatenate(outs, axis=0).T.astype(BF16)


def _attention(qt, k, aug, vt, *, tq, n_heads, n_streams):
    bsz, d, s = qt.shape
    nk, tk = vt.shape[1], vt.shape[3]
    assert tq == tk, "the causal mask is applied to the diagonal tile only"
    blk = HEADS_PER_STEP * HEAD_DIM
    wide = n_streams * blk
    lanes = aug.shape[2]
    return pl.pallas_call(
        functools.partial(_attn_kernel, tq=tq, tk=tk, n_heads=n_heads, n_streams=n_streams),
        grid=(bsz, d // wide, s // tq),
        in_specs=[
            pl.BlockSpec((1, wide, tq), lambda b, g, i: (b, g, i)),
            pl.BlockSpec((1, s, wide), lambda b, g, i: (b, 0, g)),
            pl.BlockSpec((1, s, lanes), lambda b, g, i: (b, 0, 0)),
            pl.BlockSpec((1, nk, wide, tk), lambda b, g, i: (b, 0, g, 0)),
        ],
        out_specs=pl.BlockSpec((1, tq, wide), lambda b, g, i: (b, i, g)),
        out_shape=jax.ShapeDtypeStruct((bsz, s, d), BF16),
        scratch_shapes=[
            pltpu.VMEM((n_streams, blk + lanes, HEADS_PER_STEP * tq), BF16),
            pltpu.VMEM((2, n_streams, tk, HEADS_PER_STEP * tq), F32),
            pltpu.VMEM((2, n_streams, 1, HEADS_PER_STEP * tq), F32),
            pltpu.VMEM((n_streams, 1, HEADS_PER_STEP * tq), F32),
            pltpu.VMEM((n_streams, HEADS_PER_STEP, HEAD_DIM + SUM_ROWS, tq), F32),
        ],
        compiler_params=_params(("parallel", "parallel", "arbitrary"), 40),
        name="attention",
    )(qt, k, aug, vt)


def kernel(x, c, w_ada, b_ada, conv_w_in, conv_w, conv_w_out, w_ada_kv, b_ada_kv, w_k, w_v, w_f, b_f,
           attn_w_q, attn_w_o, ffn_w_gate, ffn_w_up, ffn_w_down, ln1_g, ln1_b, ln2_g, ln2_b):
    depth = w_ada.shape[0]
    n_a = conv_w_in.shape[0]
    d = x.shape[-1]
    n_heads = w_f.shape[1]
    assert d // n_heads == HEAD_DIM and 3 * n_heads <= V7X_LANES
    alpha = (2.0 * depth) ** 0.25
    tm = 256

    mods, modkv = _mods(c, w_ada, b_ada, w_ada_kv, b_ada_kv)
    bf = lambda w: w.astype(BF16)
    w_f_pad = jnp.pad(w_f, ((0, 0), (0, V7X_LANES - n_heads))).astype(BF16)
    b_f_pad = jnp.pad(b_f, (0, V7X_LANES - n_heads)).reshape(1, V7X_LANES)

    kv = None
    for l in range(depth):
        ln = jnp.stack([ln1_g[l], ln1_b[l], ln2_g[l], ln2_b[l]])
        if l < n_a:
            z = _conv_front(x, mods[l], bf(conv_w_in[l]), conv_w[l], tm=tm)
            w_proj = conv_w_out[l]
        else:
            j = l - n_a
            if kv is None:
                qt, k, vt, aug = _qkv(x, mods[l], modkv, bf(w_k), bf(w_v.T), bf(attn_w_q[j].T), w_f_pad, b_f_pad,
                                      tm=tm, n_heads=n_heads)
                kv = (k, aug, vt)
            else:
                raise NotImplementedError("more than one attention layer needs a q-only projection kernel")
            z = _attention(qt, *kv, tq=tm, n_heads=n_heads, n_streams=2)
            w_proj = attn_w_o[j]
        x = _tail(x, z, mods[l], bf(w_proj), bf(ffn_w_gate[l]), bf(ffn_w_up[l]), bf(ffn_w_down[l]), ln,
                  alpha=alpha, tm=tm)
    return x
```

```python
import functools

import jax
import jax.numpy as jnp
from jax import lax
from jax.experimental import pallas as pl
from jax.experimental.pallas import tpu as pltpu

F32 = jnp.float32
BF16 = jnp.bfloat16

LN_EPS = 1e-5
HEAD_DIM = 64
N_MOD = 6
V7X_LANES = 128
V7X_VMEM_BYTES = 64 * 2**20
NEG_BIG = -1e30
LOG2_E = 1.4426950408889634
HEADS_PER_STEP = 2
SUM_ROWS = 16
ROW_CHUNK = 256


def _dot(a, b):
    return jnp.dot(a, b, preferred_element_type=F32)


def _dot_nt(a, b):
    return lax.dot_general(a, b, (((1,), (1,)), ((), ())), preferred_element_type=F32)


def _split3(x):
    hi = x.astype(BF16)
    r = x - hi.astype(F32)
    mid = r.astype(BF16)
    lo = (r - mid.astype(F32)).astype(BF16)
    return hi, mid, lo


def _layer_norm(x, g, b):
    mu = jnp.mean(x, axis=-1, keepdims=True)
    xc = x - mu
    var = jnp.mean(xc * xc, axis=-1, keepdims=True)
    return xc * lax.rsqrt(var + LN_EPS) * g + b


def _silu(x):
    return x / (1.0 + jnp.exp(-x))


def _const_spec(shape):
    nd = len(shape)
    return pl.BlockSpec(shape, lambda *_: (0,) * nd, pipeline_mode=pl.Buffered(1))


def _params(sem, vmem_mb):
    return pltpu.CompilerParams(dimension_semantics=sem, vmem_limit_bytes=vmem_mb * 2**20)


def _mods_kernel(c_ref, wada_ref, wkv_ref, b_ref, o_ref, *, n_ada):
    n = pl.program_id(0)
    ca = _silu(c_ref[...])
    ca_hi = ca.astype(BF16)
    ca_lo = (ca - ca_hi.astype(F32)).astype(BF16)

    def compute(w):
        w_hi = w.astype(BF16)
        w_lo = (w - w_hi.astype(F32)).astype(BF16)
        return _dot(ca_hi, w_hi) + _dot(ca_lo, w_hi) + _dot(ca_hi, w_lo)

    @pl.when(n < n_ada)
    def _():
        o_ref[0] = compute(wada_ref[0]) + b_ref[0]

    @pl.when(n >= n_ada)
    def _():
        o_ref[0] = compute(wkv_ref[...]) + b_ref[0]


def _mods(c, w_ada, b_ada, w_ada_kv, b_ada_kv):
    bsz, d = c.shape
    depth = w_ada.shape[0]
    n_ada = depth * N_MOD
    n_tot = n_ada + 2
    b_all = jnp.concatenate([b_ada.reshape(n_ada, d), b_ada_kv.reshape(2, d)], axis=0).reshape(n_tot, 1, d)
    out = pl.pallas_call(
        functools.partial(_mods_kernel, n_ada=n_ada),
        grid=(n_tot,),
        in_specs=[
            pl.BlockSpec((bsz, d), lambda n: (0, 0)),
            pl.BlockSpec((1, d, d), lambda n: (jnp.minimum(n, n_ada - 1) // N_MOD, 0, jnp.minimum(n, n_ada - 1) % N_MOD)),
            pl.BlockSpec((d, d), lambda n: (0, jnp.maximum(n - n_ada, 0))),
            pl.BlockSpec((1, 1, d), lambda n: (n, 0, 0)),
        ],
        out_specs=pl.BlockSpec((1, bsz, d), lambda n: (n, 0, 0)),
        out_shape=jax.ShapeDtypeStruct((n_tot, bsz, d), F32),
        compiler_params=_params(("arbitrary",), 40),
        name="mods",
    )(c, w_ada, w_ada_kv, b_all)
    mods = jnp.transpose(out, (1, 0, 2))
    per_layer = [mods[:, l * N_MOD:(l + 1) * N_MOD] for l in range(depth)]
    return per_layer, mods[:, n_ada:]


def _conv_kernel(x_ref, mod_ref, win_ref, cw_ref, z_ref, carry_ref, *, tm, d):
    @pl.when(pl.program_id(1) == 0)
    def _():
        carry_ref[...] = jnp.zeros_like(carry_ref)

    mod = mod_ref[0]
    h = (x_ref[0] * (1.0 + mod[1:2]) + mod[0:1]).astype(BF16)
    gc = _dot(h, win_ref[:, d:2 * d])
    v = _dot(h, win_ref[:, 2 * d:3 * d])
    u = gc * v
    row = lax.broadcasted_iota(jnp.int32, (tm, 1), 0)
    c_last = carry_ref[7:8, :]
    c_prev = carry_ref[6:7, :]
    um1 = jnp.where(row == 0, c_last, pltpu.roll(u, 1, axis=0))
    um2 = jnp.where(row == 0, c_prev, jnp.where(row == 1, c_last, pltpu.roll(u, 2, axis=0)))
    cw = cw_ref[...]
    conv = cw[0:1] * um2 + cw[1:2] * um1 + cw[2:3] * u
    carry_ref[...] = u[tm - 8:tm, :]
    gb = _dot(h, win_ref[:, 0:d])
    z_ref[0] = (gb * conv).astype(BF16)


def _conv_front(x, mod, w_in, conv_w, *, tm):
    bsz, s, d = x.shape
    return pl.pallas_call(
        functools.partial(_conv_kernel, tm=tm, d=d),
        grid=(bsz, s // tm),
        in_specs=[
            pl.BlockSpec((1, tm, d), lambda b, t: (b, t, 0)),
            pl.BlockSpec((1, N_MOD, d), lambda b, t: (b, 0, 0)),
            _const_spec(w_in.shape),
            _const_spec(conv_w.shape),
        ],
        out_specs=pl.BlockSpec((1, tm, d), lambda b, t: (b, t, 0)),
        out_shape=jax.ShapeDtypeStruct((bsz, s, d), BF16),
        scratch_shapes=[pltpu.VMEM((8, d), F32)],
        compiler_params=_params(("parallel", "arbitrary"), 48),
        name="conv_front",
    )(x, mod, w_in, conv_w)


def _tail_kernel(x_ref, z_ref, mod_ref, wp_ref, wg_ref, wu_ref, wd_ref, ln_ref, o_ref, *, alpha, tm):
    mod = mod_ref[0]
    ln = ln_ref[...]
    rows = [pl.ds(r, ROW_CHUNK) for r in range(0, tm, ROW_CHUNK)]
    ys = [_dot(z_ref[0, r, :], wp_ref[...]) for r in rows]
    x1s = [_layer_norm(alpha * x_ref[0, r, :] + mod[2:3] * y, ln[0:1], ln[1:2]) for r, y in zip(rows, ys)]
    hs = [(x1 * (1.0 + mod[4:5]) + mod[3:4]).astype(BF16) for x1 in x1s]
    gus = [(_dot(h, wg_ref[...]), _dot(h, wu_ref[...])) for h in hs]
    acts = [(_silu(g) * u).astype(BF16) for g, u in gus]
    y2s = [_dot(a, wd_ref[...]) for a in acts]
    for r, x1, y2 in zip(rows, x1s, y2s):
        o_ref[0, r, :] = _layer_norm(alpha * x1 + mod[5:6] * y2, ln[2:3], ln[3:4])


def _tail(x, z, mod, w_proj, w_gate, w_up, w_down, ln, *, alpha, tm):
    bsz, s, d = x.shape
    return pl.pallas_call(
        functools.partial(_tail_kernel, alpha=alpha, tm=tm),
        grid=(bsz, s // tm),
        in_specs=[
            pl.BlockSpec((1, tm, d), lambda b, t: (b, t, 0)),
            pl.BlockSpec((1, tm, d), lambda b, t: (b, t, 0)),
            pl.BlockSpec((1, N_MOD, d), lambda b, t: (b, 0, 0)),
            _const_spec(w_proj.shape),
            _const_spec(w_gate.shape),
            _const_spec(w_up.shape),
            _const_spec(w_down.shape),
            _const_spec(ln.shape),
        ],
        out_specs=pl.BlockSpec((1, tm, d), lambda b, t: (b, t, 0)),
        out_shape=jax.ShapeDtypeStruct((bsz, s, d), F32),
        compiler_params=_params(("parallel", "parallel"), 60),
        name="tail",
    )(x, z, mod, w_proj, w_gate, w_up, w_down, ln)


def _qkv_kernel(x_ref, mod_ref, modkv_ref, wk_ref, wvt_ref, wqt_ref, wf_ref, bf_ref,
                qt_ref, k_ref, vt_ref, aug_ref, carry_ref, *, tm, n_heads, q_scale):
    @pl.when(pl.program_id(1) == 0)
    def _():
        carry_ref[...] = jnp.zeros_like(carry_ref)

    x = x_ref[0]
    mod = mod_ref[0]
    modkv = modkv_ref[0]
    hkv = (x * (1.0 + modkv[1:2]) + modkv[0:1]).astype(BF16)
    hq = (x * (1.0 + mod[1:2]) + mod[0:1]).astype(BF16)
    k_ref[0] = _dot(hkv, wk_ref[...]).astype(BF16)
    vt_ref[0, 0] = _dot_nt(wvt_ref[...], hkv).astype(BF16)
    qt_ref[0] = (_dot_nt(wqt_ref[...], hq) * q_scale).astype(BF16)

    zf = _dot(hkv, wf_ref[...]) + bf_ref[...]
    lf = jnp.minimum(zf, 0.0) - jnp.log1p(jnp.exp(-jnp.abs(zf)))
    lane = lax.broadcasted_iota(jnp.int32, lf.shape, 1)
    lf = jnp.where(lane < n_heads, lf, 0.0)
    tri = jnp.where(lax.broadcasted_iota(jnp.int32, (tm, tm), 0) >= lax.broadcasted_iota(jnp.int32, (tm, tm), 1),
                    1.0, 0.0).astype(BF16)
    hi, mid, lo = _split3(lf)
    fcum = _dot(tri, hi) + _dot(tri, mid) + _dot(tri, lo) + carry_ref[0:1, :]
    carry_ref[0:1, :] = fcum[tm - 1:tm, :]
    a_hi, a_mid, a_lo = _split3(-LOG2_E * fcum)
    aug = (a_hi.astype(F32) + pltpu.roll(a_mid.astype(F32), n_heads, axis=1)
           + pltpu.roll(a_lo.astype(F32), 2 * n_heads, axis=1))
    aug_ref[0] = aug.astype(BF16)


def _qkv(x, mod, modkv, w_k, w_vt, w_qt, w_f, b_f, *, tm, n_heads):
    bsz, s, d = x.shape
    nt = s // tm
    lanes = w_f.shape[1]
    return pl.pallas_call(
        functools.partial(_qkv_kernel, tm=tm, n_heads=n_heads, q_scale=LOG2_E * HEAD_DIM ** -0.5),
        grid=(bsz, nt),
        in_specs=[
            pl.BlockSpec((1, tm, d), lambda b, t: (b, t, 0)),
            pl.BlockSpec((1, N_MOD, d), lambda b, t: (b, 0, 0)),
            pl.BlockSpec((1, 2, d), lambda b, t: (b, 0, 0)),
            _const_spec(w_k.shape),
            _const_spec(w_vt.shape),
            _const_spec(w_qt.shape),
            _const_spec(w_f.shape),
            _const_spec(b_f.shape),
        ],
        out_specs=[
            pl.BlockSpec((1, d, tm), lambda b, t: (b, 0, t)),
            pl.BlockSpec((1, tm, d), lambda b, t: (b, t, 0)),
            pl.BlockSpec((1, 1, d, tm), lambda b, t: (b, t, 0, 0)),
            pl.BlockSpec((1, tm, lanes), lambda b, t: (b, t, 0)),
        ],
        out_shape=[
            jax.ShapeDtypeStruct((bsz, d, s), BF16),
            jax.ShapeDtypeStruct((bsz, s, d), BF16),
            jax.ShapeDtypeStruct((bsz, nt, d, tm), BF16),
            jax.ShapeDtypeStruct((bsz, s, lanes), BF16),
        ],
        scratch_shapes=[pltpu.VMEM((8, lanes), F32)],
        compiler_params=_params(("parallel", "arbitrary"), 48),
        name="qkv",
    )(x, mod, modkv, w_k, w_vt, w_qt, w_f, b_f)


def _attn_kernel(qt_ref, k_ref, aug_ref, vt_ref, o_ref, rhs_ref, s_ref, mt_ref, m_ref, acc_ref,
                 *, tq, tk, n_heads, n_streams):
    g = pl.program_id(1)
    i = pl.program_id(2)
    hd = HEAD_DIM
    blk = HEADS_PER_STEP * hd

    r = lax.broadcasted_iota(jnp.int32, (blk, tq), 0)
    for u in range(n_streams):
        qt = qt_ref[0, u * blk:(u + 1) * blk, :].astype(F32)
        for hh in range(HEADS_PER_STEP):
            head = (g * n_streams + u) * HEADS_PER_STEP + hh
            sel = (r == head) | (r == head + n_heads) | (r == head + 2 * n_heads)
            rhs_ref[u, 0:blk, hh * tq:(hh + 1) * tq] = jnp.where(
                (r >= hh * hd) & (r < (hh + 1) * hd), qt, 0.0).astype(BF16)
            rhs_ref[u, blk:2 * blk, hh * tq:(hh + 1) * tq] = jnp.where(sel, 1.0, 0.0).astype(BF16)

    m_ref[...] = jnp.full_like(m_ref, NEG_BIG)
    acc_ref[...] = jnp.zeros_like(acc_ref)
    ones = jnp.ones((SUM_ROWS, tk), BF16)

    def qk_stage(u, j, buf, masked=False):
        ks = pl.multiple_of(j * tk, tk)
        kx = jnp.concatenate([k_ref[0, pl.ds(ks, tk), u * blk:(u + 1) * blk], aug_ref[0, pl.ds(ks, tk), :]], axis=1)
        s = _dot(kx, rhs_ref[u])
        if masked:
            krow = lax.broadcasted_iota(jnp.int32, s.shape, 0)
            qcol = lax.broadcasted_iota(jnp.int32, s.shape, 1)
            qcol = jnp.where(qcol >= tq, qcol - tq, qcol)
            s = jnp.where(krow <= qcol, s, NEG_BIG)
        s_ref[buf, u] = s
        mt_ref[buf, u] = jnp.max(s, axis=0, keepdims=True)

    def pv_stage(u, j, buf):
        m_old = m_ref[u]
        m_new = jnp.maximum(m_old, mt_ref[buf, u])
        alpha = jnp.exp2(m_old - m_new)
        pr = jnp.exp2(s_ref[buf, u] - m_new).astype(BF16)
        m_ref[u] = m_new
        for hh in range(HEADS_PER_STEP):
            row0 = u * blk + hh * hd
            vx = jnp.concatenate([vt_ref[0, j, row0:row0 + hd, :], ones], axis=0)
            pv = _dot(vx, pr[:, hh * tq:(hh + 1) * tq])
            acc_ref[u, hh] = alpha[:, hh * tq:(hh + 1) * tq] * acc_ref[u, hh] + pv

    def all_streams(fn, *args, **kwargs):
        for u in range(n_streams):
            fn(u, *args, **kwargs)

    def step(t_next, buf_next, masked=False):
        for u in range(n_streams):
            qk_stage(u, t_next, buf_next, masked=masked)
            pv_stage(u, t_next - 1, 1 - buf_next)

    odd = i % 2 == 1

    @pl.when(i == 0)
    def _():
        all_streams(qk_stage, 0, 0, masked=True)

    @pl.when(odd)
    def _():
        all_streams(qk_stage, 0, 1)

    @pl.when((i > 0) & jnp.logical_not(odd))
    def _():
        all_streams(qk_stage, 0, 0)
        step(1, 1)

    t_start = 1 - i % 2

    def body(jj, carry):
        t = t_start + 2 * jj
        step(t + 1, 0)
        step(t + 2, 1)
        return carry

    lax.fori_loop(0, jnp.maximum(i - 1, 0) // 2, body, 0)

    @pl.when(i > 0)
    def _():
        step(i, 0, masked=True)

    all_streams(pv_stage, i, 0)

    outs = []
    for u in range(n_streams):
        for hh in range(HEADS_PER_STEP):
            acc = acc_ref[u, hh]
            outs.append(acc[0:hd] / acc[hd:hd + 1])
    o_ref[0] = jnp.concatenate(outs, axis=0).T.astype(BF16)


def _attention(qt, k, aug, vt, *, tq, n_heads, n_streams):
    bsz, d, s = qt.shape
    nk, tk = vt.shape[1], vt.shape[3]
    assert tq == tk, "the causal mask is applied to the diagonal tile only"
    blk = HEADS_PER_STEP * HEAD_DIM
    wide = n_streams * blk
    lanes = aug.shape[2]
    return pl.pallas_call(
        functools.partial(_attn_kernel, tq=tq, tk=tk, n_heads=n_heads, n_streams=n_streams),
        grid=(bsz, d // wide, s // tq),
        in_specs=[
            pl.BlockSpec((1, wide, tq), lambda b, g, i: (b, g, i)),
            pl.BlockSpec((1, s, wide), lambda b, g, i: (b, 0, g)),
            pl.BlockSpec((1, s, lanes), lambda b, g, i: (b, 0, 0)),
            pl.BlockSpec((1, nk, wide, tk), lambda b, g, i: (b, 0, g, 0)),
        ],
        out_specs=pl.BlockSpec((1, tq, wide), lambda b, g, i: (b, i, g)),
        out_shape=jax.ShapeDtypeStruct((bsz, s, d), BF16),
        scratch_shapes=[
            pltpu.VMEM((n_streams, blk + lanes, HEADS_PER_STEP * tq), BF16),
            pltpu.VMEM((2, n_streams, tk, HEADS_PER_STEP * tq), F32),
            pltpu.VMEM((2, n_streams, 1, HEADS_PER_STEP * tq), F32),
            pltpu.VMEM((n_streams, 1, HEADS_PER_STEP * tq), F32),
            pltpu.VMEM((n_streams, HEADS_PER_STEP, HEAD_DIM + SUM_ROWS, tq), F32),
        ],
        compiler_params=_params(("parallel", "parallel", "arbitrary"), 40),
        name="attention",
    )(qt, k, aug, vt)


def kernel(x, c, w_ada, b_ada, conv_w_in, conv_w, conv_w_out, w_ada_kv, b_ada_kv, w_k, w_v, w_f, b_f,
           attn_w_q, attn_w_o, ffn_w_gate, ffn_w_up, ffn_w_down, ln1_g, ln1_b, ln2_g, ln2_b):
    depth = w_ada.shape[0]
    n_a = conv_w_in.shape[0]
    d = x.shape[-1]
    n_heads = w_f.shape[1]
    assert d // n_heads == HEAD_DIM and 3 * n_heads <= V7X_LANES
    alpha = (2.0 * depth) ** 0.25
    tm = ROW_CHUNK
    tm_tail = 2 * ROW_CHUNK

    mods, modkv = _mods(c, w_ada, b_ada, w_ada_kv, b_ada_kv)
    bf = lambda w: w.astype(BF16)
    w_f_pad = jnp.pad(w_f, ((0, 0), (0, V7X_LANES - n_heads))).astype(BF16)
    b_f_pad = jnp.pad(b_f, (0, V7X_LANES - n_heads)).reshape(1, V7X_LANES)

    kv = None
    for l in range(depth):
        ln = jnp.stack([ln1_g[l], ln1_b[l], ln2_g[l], ln2_b[l]])
        if l < n_a:
            z = _conv_front(x, mods[l], bf(conv_w_in[l]), conv_w[l], tm=tm)
            w_proj = conv_w_out[l]
        else:
            j = l - n_a
            if kv is None:
                qt, k, vt, aug = _qkv(x, mods[l], modkv, bf(w_k), bf(w_v.T), bf(attn_w_q[j].T), w_f_pad, b_f_pad,
                                      tm=tm, n_heads=n_heads)
                kv = (k, aug, vt)
            else:
                raise NotImplementedError("more than one attention layer needs a q-only projection kernel")
            z = _attention(qt, *kv, tq=tm, n_heads=n_heads, n_streams=4)
            w_proj = attn_w_o[j]
        x = _tail(x, z, mods[l], bf(w_proj), bf(ffn_w_gate[l]), bf(ffn_w_up[l]), bf(ffn_w_down[l]), ln,
                  alpha=alpha, tm=tm_tail)
    return x
```

```python
import functools

import jax
import jax.numpy as jnp
from jax import lax
from jax.experimental import pallas as pl
from jax.experimental.pallas import tpu as pltpu

F32 = jnp.float32
BF16 = jnp.bfloat16

LN_EPS = 1e-5
HEAD_DIM = 64
N_MOD = 6
V7X_LANES = 128
V7X_VMEM_BYTES = 64 * 2**20
NEG_BIG = -1e30
LOG2_E = 1.4426950408889634
HEADS_PER_STEP = 2
SUM_ROWS = 16
ROW_CHUNK = 256


def _dot(a, b):
    return jnp.dot(a, b, preferred_element_type=F32)


def _dot_nt(a, b):
    return lax.dot_general(a, b, (((1,), (1,)), ((), ())), preferred_element_type=F32)


def _split3(x):
    hi = x.astype(BF16)
    r = x - hi.astype(F32)
    mid = r.astype(BF16)
    lo = (r - mid.astype(F32)).astype(BF16)
    return hi, mid, lo


def _layer_norm(x, g, b):
    mu = jnp.mean(x, axis=-1, keepdims=True)
    xc = x - mu
    var = jnp.mean(xc * xc, axis=-1, keepdims=True)
    return xc * lax.rsqrt(var + LN_EPS) * g + b


def _silu(x):
    return x / (1.0 + jnp.exp(-x))


def _const_spec(shape):
    nd = len(shape)
    return pl.BlockSpec(shape, lambda *_: (0,) * nd, pipeline_mode=pl.Buffered(1))


def _params(sem, vmem_mb):
    return pltpu.CompilerParams(dimension_semantics=sem, vmem_limit_bytes=vmem_mb * 2**20)


def _mods_kernel(c_ref, wada_ref, wkv_ref, b_ref, o_ref, *, n_ada):
    n = pl.program_id(0)
    ca = _silu(c_ref[...])
    ca_hi = ca.astype(BF16)
    ca_lo = (ca - ca_hi.astype(F32)).astype(BF16)

    def compute(w):
        w_hi = w.astype(BF16)
        w_lo = (w - w_hi.astype(F32)).astype(BF16)
        return _dot(ca_hi, w_hi) + _dot(ca_lo, w_hi) + _dot(ca_hi, w_lo)

    @pl.when(n < n_ada)
    def _():
        o_ref[0] = compute(wada_ref[0]) + b_ref[0]

    @pl.when(n >= n_ada)
    def _():
        o_ref[0] = compute(wkv_ref[...]) + b_ref[0]


def _mods(c, w_ada, b_ada, w_ada_kv, b_ada_kv):
    bsz, d = c.shape
    depth = w_ada.shape[0]
    n_ada = depth * N_MOD
    n_tot = n_ada + 2
    b_all = jnp.concatenate([b_ada.reshape(n_ada, d), b_ada_kv.reshape(2, d)], axis=0).reshape(n_tot, 1, d)
    out = pl.pallas_call(
        functools.partial(_mods_kernel, n_ada=n_ada),
        grid=(n_tot,),
        in_specs=[
            pl.BlockSpec((bsz, d), lambda n: (0, 0)),
            pl.BlockSpec((1, d, d), lambda n: (jnp.minimum(n, n_ada - 1) // N_MOD, 0, jnp.minimum(n, n_ada - 1) % N_MOD)),
            pl.BlockSpec((d, d), lambda n: (0, jnp.maximum(n - n_ada, 0))),
            pl.BlockSpec((1, 1, d), lambda n: (n, 0, 0)),
        ],
        out_specs=pl.BlockSpec((1, bsz, d), lambda n: (n, 0, 0)),
        out_shape=jax.ShapeDtypeStruct((n_tot, bsz, d), F32),
        compiler_params=_params(("arbitrary",), 40),
        name="mods",
    )(c, w_ada, w_ada_kv, b_all)
    mods = jnp.transpose(out, (1, 0, 2))
    per_layer = [mods[:, l * N_MOD:(l + 1) * N_MOD] for l in range(depth)]
    return per_layer, mods[:, n_ada:]


def _conv_kernel(x_ref, mod_ref, win_ref, cw_ref, z_ref, carry_ref, *, tm, d):
    @pl.when(pl.program_id(1) == 0)
    def _():
        carry_ref[...] = jnp.zeros_like(carry_ref)

    mod = mod_ref[0]
    h = (x_ref[0] * (1.0 + mod[1:2]) + mod[0:1]).astype(BF16)
    gc = _dot(h, win_ref[:, d:2 * d])
    v = _dot(h, win_ref[:, 2 * d:3 * d])
    u = gc * v
    row = lax.broadcasted_iota(jnp.int32, (tm, 1), 0)
    c_last = carry_ref[7:8, :]
    c_prev = carry_ref[6:7, :]
    um1 = jnp.where(row == 0, c_last, pltpu.roll(u, 1, axis=0))
    um2 = jnp.where(row == 0, c_prev, jnp.where(row == 1, c_last, pltpu.roll(u, 2, axis=0)))
    cw = cw_ref[...]
    conv = cw[0:1] * um2 + cw[1:2] * um1 + cw[2:3] * u
    carry_ref[...] = u[tm - 8:tm, :]
    gb = _dot(h, win_ref[:, 0:d])
    z_ref[0] = (gb * conv).astype(BF16)


def _conv_front(x, mod, w_in, conv_w, *, tm):
    bsz, s, d = x.shape
    return pl.pallas_call(
        functools.partial(_conv_kernel, tm=tm, d=d),
        grid=(bsz, s // tm),
        in_specs=[
            pl.BlockSpec((1, tm, d), lambda b, t: (b, t, 0)),
            pl.BlockSpec((1, N_MOD, d), lambda b, t: (b, 0, 0)),
            _const_spec(w_in.shape),
            _const_spec(conv_w.shape),
        ],
        out_specs=pl.BlockSpec((1, tm, d), lambda b, t: (b, t, 0)),
        out_shape=jax.ShapeDtypeStruct((bsz, s, d), BF16),
        scratch_shapes=[pltpu.VMEM((8, d), F32)],
        compiler_params=_params(("parallel", "arbitrary"), 48),
        name="conv_front",
    )(x, mod, w_in, conv_w)


def _tail_kernel(x_ref, z_ref, mod_ref, wp_ref, wg_ref, wu_ref, wd_ref, ln_ref, o_ref, *, alpha, tm):
    mod = mod_ref[0]
    ln = ln_ref[...]
    rows = [pl.ds(r, ROW_CHUNK) for r in range(0, tm, ROW_CHUNK)]
    ys = [_dot(z_ref[0, r, :], wp_ref[...]) for r in rows]
    x1s = [_layer_norm(alpha * x_ref[0, r, :] + mod[2:3] * y, ln[0:1], ln[1:2]) for r, y in zip(rows, ys)]
    hs = [(x1 * (1.0 + mod[4:5]) + mod[3:4]).astype(BF16) for x1 in x1s]
    gus = [(_dot(h, wg_ref[...]), _dot(h, wu_ref[...])) for h in hs]
    acts = [(_silu(g) * u).astype(BF16) for g, u in gus]
    y2s = [_dot(a, wd_ref[...]) for a in acts]
    for r, x1, y2 in zip(rows, x1s, y2s):
        o_ref[0, r, :] = _layer_norm(alpha * x1 + mod[5:6] * y2, ln[2:3], ln[3:4])


def _tail(x, z, mod, w_proj, w_gate, w_up, w_down, ln, *, alpha, tm):
    bsz, s, d = x.shape
    return pl.pallas_call(
        functools.partial(_tail_kernel, alpha=alpha, tm=tm),
        grid=(bsz, s // tm),
        in_specs=[
            pl.BlockSpec((1, tm, d), lambda b, t: (b, t, 0)),
            pl.BlockSpec((1, tm, d), lambda b, t: (b, t, 0)),
            pl.BlockSpec((1, N_MOD, d), lambda b, t: (b, 0, 0)),
            _const_spec(w_proj.shape),
            _const_spec(w_gate.shape),
            _const_spec(w_up.shape),
            _const_spec(w_down.shape),
            _const_spec(ln.shape),
        ],
        out_specs=pl.BlockSpec((1, tm, d), lambda b, t: (b, t, 0)),
        out_shape=jax.ShapeDtypeStruct((bsz, s, d), F32),
        compiler_params=_params(("parallel", "parallel"), 60),
        name="tail",
    )(x, z, mod, w_proj, w_gate, w_up, w_down, ln)


def _qkv_kernel(x_ref, mod_ref, modkv_ref, wk_ref, wvt_ref, wqt_ref, wf_ref, bf_ref,
                qt_ref, k_ref, vt_ref, aug_ref, carry_ref, *, tm, n_heads, q_scale):
    @pl.when(pl.program_id(1) == 0)
    def _():
        carry_ref[...] = jnp.zeros_like(carry_ref)

    x = x_ref[0]
    mod = mod_ref[0]
    modkv = modkv_ref[0]
    hkv = (x * (1.0 + modkv[1:2]) + modkv[0:1]).astype(BF16)
    hq = (x * (1.0 + mod[1:2]) + mod[0:1]).astype(BF16)
    zf = _dot(hkv, wf_ref[...]) + bf_ref[...]
    k_ref[0] = _dot(hkv, wk_ref[...]).astype(BF16)
    vt_ref[0, 0] = _dot_nt(wvt_ref[...], hkv).astype(BF16)
    qt_ref[0] = (_dot_nt(wqt_ref[...], hq) * q_scale).astype(BF16)

    lf = jnp.minimum(zf, 0.0) - jnp.log1p(jnp.exp(-jnp.abs(zf)))
    lane = lax.broadcasted_iota(jnp.int32, lf.shape, 1)
    lf = jnp.where(lane < n_heads, lf, 0.0)
    tri = jnp.where(lax.broadcasted_iota(jnp.int32, (tm, tm), 0) >= lax.broadcasted_iota(jnp.int32, (tm, tm), 1),
                    1.0, 0.0).astype(BF16)
    hi, mid, lo = _split3(lf)
    fcum = _dot(tri, hi) + _dot(tri, mid) + _dot(tri, lo) + carry_ref[0:1, :]
    carry_ref[0:1, :] = fcum[tm - 1:tm, :]
    a_hi, a_mid, a_lo = _split3(-LOG2_E * fcum)
    aug = (a_hi.astype(F32) + pltpu.roll(a_mid.astype(F32), n_heads, axis=1)
           + pltpu.roll(a_lo.astype(F32), 2 * n_heads, axis=1))
    aug_ref[0] = aug.astype(BF16)


def _qkv(x, mod, modkv, w_k, w_vt, w_qt, w_f, b_f, *, tm, n_heads):
    bsz, s, d = x.shape
    nt = s // tm
    lanes = w_f.shape[1]
    return pl.pallas_call(
        functools.partial(_qkv_kernel, tm=tm, n_heads=n_heads, q_scale=LOG2_E * HEAD_DIM ** -0.5),
        grid=(bsz, nt),
        in_specs=[
            pl.BlockSpec((1, tm, d), lambda b, t: (b, t, 0)),
            pl.BlockSpec((1, N_MOD, d), lambda b, t: (b, 0, 0)),
            pl.BlockSpec((1, 2, d), lambda b, t: (b, 0, 0)),
            _const_spec(w_k.shape),
            _const_spec(w_vt.shape),
            _const_spec(w_qt.shape),
            _const_spec(w_f.shape),
            _const_spec(b_f.shape),
        ],
        out_specs=[
            pl.BlockSpec((1, d, tm), lambda b, t: (b, 0, t)),
            pl.BlockSpec((1, tm, d), lambda b, t: (b, t, 0)),
            pl.BlockSpec((1, 1, d, tm), lambda b, t: (b, t, 0, 0)),
            pl.BlockSpec((1, tm, lanes), lambda b, t: (b, t, 0)),
        ],
        out_shape=[
            jax.ShapeDtypeStruct((bsz, d, s), BF16),
            jax.ShapeDtypeStruct((bsz, s, d), BF16),
            jax.ShapeDtypeStruct((bsz, nt, d, tm), BF16),
            jax.ShapeDtypeStruct((bsz, s, lanes), BF16),
        ],
        scratch_shapes=[pltpu.VMEM((8, lanes), F32)],
        compiler_params=_params(("parallel", "arbitrary"), 48),
        name="qkv",
    )(x, mod, modkv, w_k, w_vt, w_qt, w_f, b_f)


def _attn_kernel(qt_ref, k_ref, aug_ref, vt_ref, o_ref, rhs_ref, s_ref, mt_ref, m_ref, acc_ref,
                 *, tq, tk, n_heads, n_streams, n_q):
    g = pl.program_id(1)
    i = pl.program_id(2)
    hd = HEAD_DIM
    blk = HEADS_PER_STEP * hd
    diag = 2
    ones = jnp.ones((SUM_ROWS, tk), BF16)

    def build_rhs():
        r = lax.broadcasted_iota(jnp.int32, (blk, tq), 0)
        for u in range(n_streams):
            qt = qt_ref[0, u * blk:(u + 1) * blk, :].astype(F32)
            for hh in range(HEADS_PER_STEP):
                head = (g * n_streams + u) * HEADS_PER_STEP + hh
                sel = (r == head) | (r == head + n_heads) | (r == head + 2 * n_heads)
                rhs_ref[u, 0:blk, hh * tq:(hh + 1) * tq] = jnp.where(
                    (r >= hh * hd) & (r < (hh + 1) * hd), qt, 0.0).astype(BF16)
                rhs_ref[u, blk:2 * blk, hh * tq:(hh + 1) * tq] = jnp.where(sel, 1.0, 0.0).astype(BF16)

    def reset_state():
        m_ref[...] = jnp.full_like(m_ref, NEG_BIG)
        acc_ref[...] = jnp.zeros_like(acc_ref)

    def finish():
        outs = []
        for u in range(n_streams):
            for hh in range(HEADS_PER_STEP):
                acc = acc_ref[u, hh]
                outs.append(acc[0:hd] / acc[hd:hd + 1])
        o_ref[0] = jnp.concatenate(outs, axis=0).T.astype(BF16)

    def qk_stage(u, j, buf, masked=False):
        ks = pl.multiple_of(j * tk, tk)
        kx = jnp.concatenate([k_ref[0, pl.ds(ks, tk), u * blk:(u + 1) * blk], aug_ref[0, pl.ds(ks, tk), :]], axis=1)
        s = _dot(kx, rhs_ref[u])
        if masked:
            krow = lax.broadcasted_iota(jnp.int32, s.shape, 0)
            qcol = lax.broadcasted_iota(jnp.int32, s.shape, 1)
            qcol = jnp.where(qcol >= tq, qcol - tq, qcol)
            s = jnp.where(krow <= qcol, s, NEG_BIG)
        s_ref[buf, u] = s
        mt_ref[buf, u] = jnp.max(s, axis=0, keepdims=True)

    def pv_stage(u, j, buf):
        m_old = m_ref[u]
        m_new = jnp.maximum(m_old, mt_ref[buf, u])
        alpha = jnp.exp2(m_old - m_new)
        pr = jnp.exp2(s_ref[buf, u] - m_new).astype(BF16)
        m_ref[u] = m_new
        for hh in range(HEADS_PER_STEP):
            row0 = u * blk + hh * hd
            vx = jnp.concatenate([vt_ref[0, j, row0:row0 + hd, :], ones], axis=0)
            pv = _dot(vx, pr[:, hh * tq:(hh + 1) * tq])
            acc_ref[u, hh] = alpha[:, hh * tq:(hh + 1) * tq] * acc_ref[u, hh] + pv

    def all_streams(fn, *args, **kwargs):
        for u in range(n_streams):
            fn(u, *args, **kwargs)

    def step(t_next, buf_next, t_cur, buf_cur, masked=False):
        for u in range(n_streams):
            qk_stage(u, t_next, buf_next, masked=masked)
            pv_stage(u, t_cur, buf_cur)

    inner = (i > 0) & (i < n_q)
    odd = i % 2 == 1

    @pl.when(i == 0)
    def _():
        build_rhs()
        reset_state()
        all_streams(qk_stage, 0, diag, masked=True)

    def start(buf_first):
        build_rhs()
        step(0, buf_first, i - 1, diag)
        finish()
        reset_state()

    @pl.when(inner & odd)
    def _():
        start(1)

    @pl.when(inner & jnp.logical_not(odd))
    def _():
        start(0)
        step(1, 1, 0, 0)

    t_start = 1 - i % 2

    def body(jj, carry):
        t = t_start + 2 * jj
        step(t + 1, 0, t, 1)
        step(t + 2, 1, t + 1, 0)
        return carry

    lax.fori_loop(0, jnp.where(inner, (i - 1) // 2, 0), body, 0)

    @pl.when(inner)
    def _():
        step(i, diag, i - 1, 1, masked=True)

    @pl.when(i == n_q)
    def _():
        all_streams(pv_stage, n_q - 1, diag)
        finish()


def _attention(qt, k, aug, vt, *, tq, n_heads, n_streams):
    bsz, d, s = qt.shape
    nk, tk = vt.shape[1], vt.shape[3]
    assert tq == tk, "the causal mask is applied to the diagonal tile only"
    blk = HEADS_PER_STEP * HEAD_DIM
    wide = n_streams * blk
    lanes = aug.shape[2]
    n_q = s // tq
    return pl.pallas_call(
        functools.partial(_attn_kernel, tq=tq, tk=tk, n_heads=n_heads, n_streams=n_streams, n_q=n_q),
        grid=(bsz, d // wide, n_q + 1),
        in_specs=[
            pl.BlockSpec((1, wide, tq), lambda b, g, i: (b, g, jnp.minimum(i, n_q - 1))),
            pl.BlockSpec((1, s, wide), lambda b, g, i: (b, 0, g)),
            pl.BlockSpec((1, s, lanes), lambda b, g, i: (b, 0, 0)),
            pl.BlockSpec((1, nk, wide, tk), lambda b, g, i: (b, 0, g, 0)),
        ],
        out_specs=pl.BlockSpec((1, tq, wide), lambda b, g, i: (b, jnp.maximum(i - 1, 0), g)),
        out_shape=jax.ShapeDtypeStruct((bsz, s, d), BF16),
        scratch_shapes=[
            pltpu.VMEM((n_streams, blk + lanes, HEADS_PER_STEP * tq), BF16),
            pltpu.VMEM((3, n_streams, tk, HEADS_PER_STEP * tq), F32),
            pltpu.VMEM((3, n_streams, 1, HEADS_PER_STEP * tq), F32),
            pltpu.VMEM((n_streams, 1, HEADS_PER_STEP * tq), F32),
            pltpu.VMEM((n_streams, HEADS_PER_STEP, HEAD_DIM + SUM_ROWS, tq), F32),
        ],
        compiler_params=_params(("parallel", "parallel", "arbitrary"), 40),
        name="attention",
    )(qt, k, aug, vt)


def kernel(x, c, w_ada, b_ada, conv_w_in, conv_w, conv_w_out, w_ada_kv, b_ada_kv, w_k, w_v, w_f, b_f,
           attn_w_q, attn_w_o, ffn_w_gate, ffn_w_up, ffn_w_down, ln1_g, ln1_b, ln2_g, ln2_b):
    depth = w_ada.shape[0]
    n_a = conv_w_in.shape[0]
    d = x.shape[-1]
    n_heads = w_f.shape[1]
    assert d // n_heads == HEAD_DIM and 3 * n_heads <= V7X_LANES
    alpha = (2.0 * depth) ** 0.25
    tm = ROW_CHUNK
    tm_tail = 2 * ROW_CHUNK

    mods, modkv = _mods(c, w_ada, b_ada, w_ada_kv, b_ada_kv)
    bf = lambda w: w.astype(BF16)
    w_f_pad = jnp.pad(w_f, ((0, 0), (0, V7X_LANES - n_heads))).astype(BF16)
    b_f_pad = jnp.pad(b_f, (0, V7X_LANES - n_heads)).reshape(1, V7X_LANES)

    kv = None
    for l in range(depth):
        ln = jnp.stack([ln1_g[l], ln1_b[l], ln2_g[l], ln2_b[l]])
        if l < n_a:
            z = _conv_front(x, mods[l], bf(conv_w_in[l]), conv_w[l], tm=tm)
            w_proj = conv_w_out[l]
        else:
            j = l - n_a
            if kv is None:
                qt, k, vt, aug = _qkv(x, mods[l], modkv, bf(w_k), bf(w_v.T), bf(attn_w_q[j].T), w_f_pad, b_f_pad,
                                      tm=tm, n_heads=n_heads)
                kv = (k, aug, vt)
            else:
                raise NotImplementedError("more than one attention layer needs a q-only projection kernel")
            z = _attention(qt, *kv, tq=tm, n_heads=n_heads, n_streams=4)
            w_proj = attn_w_o[j]
        x = _tail(x, z, mods[l], bf(w_proj), bf(ffn_w_gate[l]), bf(ffn_w_up[l]), bf(ffn_w_down[l]), ln,
                  alpha=alpha, tm=tm_tail)
    return x
```

```python
import functools

import jax
import jax.numpy as jnp
from jax import lax
from jax.experimental import pallas as pl
from jax.experimental.pallas import tpu as pltpu

F32 = jnp.float32
BF16 = jnp.bfloat16

LN_EPS = 1e-5
HEAD_DIM = 64
N_MOD = 6
V7X_LANES = 128
V7X_VMEM_BYTES = 64 * 2**20
NEG_BIG = -1e30
LOG2_E = 1.4426950408889634
HEADS_PER_STEP = 2
SUM_ROWS = 16
ROW_CHUNK = 256


def _dot(a, b):
    return jnp.dot(a, b, preferred_element_type=F32)


def _dot_nt(a, b):
    return lax.dot_general(a, b, (((1,), (1,)), ((), ())), preferred_element_type=F32)


def _split3(x):
    hi = x.astype(BF16)
    r = x - hi.astype(F32)
    mid = r.astype(BF16)
    lo = (r - mid.astype(F32)).astype(BF16)
    return hi, mid, lo


def _layer_norm(x, g, b):
    mu = jnp.mean(x, axis=-1, keepdims=True)
    xc = x - mu
    var = jnp.mean(xc * xc, axis=-1, keepdims=True)
    return xc * lax.rsqrt(var + LN_EPS) * g + b


def _silu(x):
    return x / (1.0 + jnp.exp(-x))


def _const_spec(shape):
    nd = len(shape)
    return pl.BlockSpec(shape, lambda *_: (0,) * nd, pipeline_mode=pl.Buffered(1))


def _params(sem, vmem_mb):
    return pltpu.CompilerParams(dimension_semantics=sem, vmem_limit_bytes=vmem_mb * 2**20)


def _mods_kernel(c_ref, wada_ref, wkv_ref, b_ref, o_ref, *, n_ada):
    n = pl.program_id(0)
    ca = _silu(c_ref[...])
    ca_hi = ca.astype(BF16)
    ca_lo = (ca - ca_hi.astype(F32)).astype(BF16)

    def compute(w):
        w_hi = w.astype(BF16)
        w_lo = (w - w_hi.astype(F32)).astype(BF16)
        return _dot(ca_hi, w_hi) + _dot(ca_lo, w_hi) + _dot(ca_hi, w_lo)

    @pl.when(n < n_ada)
    def _():
        o_ref[0] = compute(wada_ref[0]) + b_ref[0]

    @pl.when(n >= n_ada)
    def _():
        o_ref[0] = compute(wkv_ref[...]) + b_ref[0]


def _mods(c, w_ada, b_ada, w_ada_kv, b_ada_kv):
    bsz, d = c.shape
    depth = w_ada.shape[0]
    n_ada = depth * N_MOD
    n_tot = n_ada + 2
    b_all = jnp.concatenate([b_ada.reshape(n_ada, d), b_ada_kv.reshape(2, d)], axis=0).reshape(n_tot, 1, d)
    out = pl.pallas_call(
        functools.partial(_mods_kernel, n_ada=n_ada),
        grid=(n_tot,),
        in_specs=[
            pl.BlockSpec((bsz, d), lambda n: (0, 0)),
            pl.BlockSpec((1, d, d), lambda n: (jnp.minimum(n, n_ada - 1) // N_MOD, 0, jnp.minimum(n, n_ada - 1) % N_MOD)),
            pl.BlockSpec((d, d), lambda n: (0, jnp.maximum(n - n_ada, 0))),
            pl.BlockSpec((1, 1, d), lambda n: (n, 0, 0)),
        ],
        out_specs=pl.BlockSpec((1, bsz, d), lambda n: (n, 0, 0)),
        out_shape=jax.ShapeDtypeStruct((n_tot, bsz, d), F32),
        compiler_params=_params(("arbitrary",), 40),
        name="mods",
    )(c, w_ada, w_ada_kv, b_all)
    mods = jnp.transpose(out, (1, 0, 2))
    per_layer = [mods[:, l * N_MOD:(l + 1) * N_MOD] for l in range(depth)]
    return per_layer, mods[:, n_ada:]


def _conv_kernel(x_ref, mod_ref, win_ref, cw_ref, z_ref, carry_ref, *, tm, d):
    @pl.when(pl.program_id(1) == 0)
    def _():
        carry_ref[...] = jnp.zeros_like(carry_ref)

    mod = mod_ref[0]
    h = (x_ref[0] * (1.0 + mod[1:2]) + mod[0:1]).astype(BF16)
    gc = _dot(h, win_ref[:, d:2 * d])
    v = _dot(h, win_ref[:, 2 * d:3 * d])
    u = gc * v
    row = lax.broadcasted_iota(jnp.int32, (tm, 1), 0)
    c_last = carry_ref[7:8, :]
    c_prev = carry_ref[6:7, :]
    um1 = jnp.where(row == 0, c_last, pltpu.roll(u, 1, axis=0))
    um2 = jnp.where(row == 0, c_prev, jnp.where(row == 1, c_last, pltpu.roll(u, 2, axis=0)))
    cw = cw_ref[...]
    conv = cw[0:1] * um2 + cw[1:2] * um1 + cw[2:3] * u
    carry_ref[...] = u[tm - 8:tm, :]
    gb = _dot(h, win_ref[:, 0:d])
    z_ref[0] = (gb * conv).astype(BF16)


def _conv_front(x, mod, w_in, conv_w, *, tm):
    bsz, s, d = x.shape
    return pl.pallas_call(
        functools.partial(_conv_kernel, tm=tm, d=d),
        grid=(bsz, s // tm),
        in_specs=[
            pl.BlockSpec((1, tm, d), lambda b, t: (b, t, 0)),
            pl.BlockSpec((1, N_MOD, d), lambda b, t: (b, 0, 0)),
            _const_spec(w_in.shape),
            _const_spec(conv_w.shape),
        ],
        out_specs=pl.BlockSpec((1, tm, d), lambda b, t: (b, t, 0)),
        out_shape=jax.ShapeDtypeStruct((bsz, s, d), BF16),
        scratch_shapes=[pltpu.VMEM((8, d), F32)],
        compiler_params=_params(("parallel", "arbitrary"), 48),
        name="conv_front",
    )(x, mod, w_in, conv_w)


def _tail_kernel(x_ref, z_ref, mod_ref, wp_ref, wg_ref, wu_ref, wd_ref, ln_ref, o_ref, *, alpha, tm):
    mod = mod_ref[0]
    ln = ln_ref[...]
    rows = [pl.ds(r, ROW_CHUNK) for r in range(0, tm, ROW_CHUNK)]
    ys = [_dot(z_ref[0, r, :], wp_ref[...]) for r in rows]
    x1s = [_layer_norm(alpha * x_ref[0, r, :] + mod[2:3] * y, ln[0:1], ln[1:2]) for r, y in zip(rows, ys)]
    hs = [(x1 * (1.0 + mod[4:5]) + mod[3:4]).astype(BF16) for x1 in x1s]
    gus = [(_dot(h, wg_ref[...]), _dot(h, wu_ref[...])) for h in hs]
    acts = [(_silu(g) * u).astype(BF16) for g, u in gus]
    y2s = [_dot(a, wd_ref[...]) for a in acts]
    for r, x1, y2 in zip(rows, x1s, y2s):
        o_ref[0, r, :] = _layer_norm(alpha * x1 + mod[5:6] * y2, ln[2:3], ln[3:4])


def _tail(x, z, mod, w_proj, w_gate, w_up, w_down, ln, *, alpha, tm):
    bsz, s, d = x.shape
    return pl.pallas_call(
        functools.partial(_tail_kernel, alpha=alpha, tm=tm),
        grid=(bsz, s // tm),
        in_specs=[
            pl.BlockSpec((1, tm, d), lambda b, t: (b, t, 0)),
            pl.BlockSpec((1, tm, d), lambda b, t: (b, t, 0)),
            pl.BlockSpec((1, N_MOD, d), lambda b, t: (b, 0, 0)),
            _const_spec(w_proj.shape),
            _const_spec(w_gate.shape),
            _const_spec(w_up.shape),
            _const_spec(w_down.shape),
            _const_spec(ln.shape),
        ],
        out_specs=pl.BlockSpec((1, tm, d), lambda b, t: (b, t, 0)),
        out_shape=jax.ShapeDtypeStruct((bsz, s, d), F32),
        compiler_params=_params(("parallel", "parallel"), 60),
        name="tail",
    )(x, z, mod, w_proj, w_gate, w_up, w_down, ln)


def _qkv_kernel(x_ref, mod_ref, modkv_ref, wk_ref, wvt_ref, wqt_ref, wf_ref, bf_ref,
                qt_ref, k_ref, vt_ref, aug_ref, carry_ref, *, tm, tk, n_heads, q_scale):
    @pl.when(pl.program_id(1) == 0)
    def _():
        carry_ref[...] = jnp.zeros_like(carry_ref)

    x = x_ref[0]
    mod = mod_ref[0]
    modkv = modkv_ref[0]
    hkv = (x * (1.0 + modkv[1:2]) + modkv[0:1]).astype(BF16)
    hq = (x * (1.0 + mod[1:2]) + mod[0:1]).astype(BF16)
    zf = _dot(hkv, wf_ref[...]) + bf_ref[...]
    k_ref[0] = _dot(hkv, wk_ref[...]).astype(BF16)
    vt = _dot_nt(wvt_ref[...], hkv).astype(BF16)
    for c in range(tm // tk):
        vt_ref[0, c] = vt[:, c * tk:(c + 1) * tk]
    qt_ref[0] = (_dot_nt(wqt_ref[...], hq) * q_scale).astype(BF16)

    lf = jnp.minimum(zf, 0.0) - jnp.log1p(jnp.exp(-jnp.abs(zf)))
    lane = lax.broadcasted_iota(jnp.int32, lf.shape, 1)
    lf = jnp.where(lane < n_heads, lf, 0.0)
    tri = jnp.where(lax.broadcasted_iota(jnp.int32, (ROW_CHUNK, ROW_CHUNK), 0)
                    >= lax.broadcasted_iota(jnp.int32, (ROW_CHUNK, ROW_CHUNK), 1), 1.0, 0.0).astype(BF16)
    total = carry_ref[0:1, :]
    for r in range(0, tm, ROW_CHUNK):
        hi, mid, lo = _split3(lf[r:r + ROW_CHUNK])
        fcum = _dot(tri, hi) + _dot(tri, mid) + _dot(tri, lo) + total
        total = fcum[ROW_CHUNK - 1:ROW_CHUNK, :]
        a_hi, a_mid, a_lo = _split3(-LOG2_E * fcum)
        aug = (a_hi.astype(F32) + pltpu.roll(a_mid.astype(F32), n_heads, axis=1)
               + pltpu.roll(a_lo.astype(F32), 2 * n_heads, axis=1))
        aug_ref[0, r:r + ROW_CHUNK, :] = aug.astype(BF16)
    carry_ref[0:1, :] = total


def _qkv(x, mod, modkv, w_k, w_vt, w_qt, w_f, b_f, *, tm, tk, n_heads):
    bsz, s, d = x.shape
    nt = s // tm
    lanes = w_f.shape[1]
    return pl.pallas_call(
        functools.partial(_qkv_kernel, tm=tm, tk=tk, n_heads=n_heads, q_scale=LOG2_E * HEAD_DIM ** -0.5),
        grid=(bsz, nt),
        in_specs=[
            pl.BlockSpec((1, tm, d), lambda b, t: (b, t, 0)),
            pl.BlockSpec((1, N_MOD, d), lambda b, t: (b, 0, 0)),
            pl.BlockSpec((1, 2, d), lambda b, t: (b, 0, 0)),
            _const_spec(w_k.shape),
            _const_spec(w_vt.shape),
            _const_spec(w_qt.shape),
            _const_spec(w_f.shape),
            _const_spec(b_f.shape),
        ],
        out_specs=[
            pl.BlockSpec((1, d, tm), lambda b, t: (b, 0, t)),
            pl.BlockSpec((1, tm, d), lambda b, t: (b, t, 0)),
            pl.BlockSpec((1, tm // tk, d, tk), lambda b, t: (b, t, 0, 0)),
            pl.BlockSpec((1, tm, lanes), lambda b, t: (b, t, 0)),
        ],
        out_shape=[
            jax.ShapeDtypeStruct((bsz, d, s), BF16),
            jax.ShapeDtypeStruct((bsz, s, d), BF16),
            jax.ShapeDtypeStruct((bsz, s // tk, d, tk), BF16),
            jax.ShapeDtypeStruct((bsz, s, lanes), BF16),
        ],
        scratch_shapes=[pltpu.VMEM((8, lanes), F32)],
        compiler_params=_params(("parallel", "arbitrary"), 48),
        name="qkv",
    )(x, mod, modkv, w_k, w_vt, w_qt, w_f, b_f)


def _attn_kernel(qt_ref, k_ref, aug_ref, vt_ref, o_ref, rhs_ref, s_ref, mt_ref, m_ref, acc_ref,
                 *, tq, tk, n_heads, n_streams, n_q):
    g = pl.program_id(1)
    i = pl.program_id(2)
    hd = HEAD_DIM
    blk = HEADS_PER_STEP * hd
    diag = 2
    ones = jnp.ones((SUM_ROWS, tk), BF16)

    def build_rhs():
        r = lax.broadcasted_iota(jnp.int32, (blk, tq), 0)
        for u in range(n_streams):
            qt = qt_ref[0, u * blk:(u + 1) * blk, :].astype(F32)
            for hh in range(HEADS_PER_STEP):
                head = (g * n_streams + u) * HEADS_PER_STEP + hh
                sel = (r == head) | (r == head + n_heads) | (r == head + 2 * n_heads)
                rhs_ref[u, 0:blk, hh * tq:(hh + 1) * tq] = jnp.where(
                    (r >= hh * hd) & (r < (hh + 1) * hd), qt, 0.0).astype(BF16)
                rhs_ref[u, blk:2 * blk, hh * tq:(hh + 1) * tq] = jnp.where(sel, 1.0, 0.0).astype(BF16)

    def reset_state():
        m_ref[...] = jnp.full_like(m_ref, NEG_BIG)
        acc_ref[...] = jnp.zeros_like(acc_ref)

    def finish():
        outs = []
        for u in range(n_streams):
            for hh in range(HEADS_PER_STEP):
                acc = acc_ref[u, hh]
                outs.append(acc[0:hd] / acc[hd:hd + 1])
        o_ref[0] = jnp.concatenate(outs, axis=0).T.astype(BF16)

    def qk_stage(u, j, buf, masked=False):
        ks = pl.multiple_of(j * tk, tk)
        kx = jnp.concatenate([k_ref[0, pl.ds(ks, tk), u * blk:(u + 1) * blk], aug_ref[0, pl.ds(ks, tk), :]], axis=1)
        s = _dot(kx, rhs_ref[u])
        if masked:
            krow = lax.broadcasted_iota(jnp.int32, s.shape, 0)
            qcol = lax.broadcasted_iota(jnp.int32, s.shape, 1)
            qcol = jnp.where(qcol >= tq, qcol - tq, qcol)
            s = jnp.where(krow <= qcol, s, NEG_BIG)
        s_ref[buf, u] = s
        mt_ref[buf, u] = jnp.max(s, axis=0, keepdims=True)

    def pv_stage(u, j, buf):
        m_old = m_ref[u]
        m_new = jnp.maximum(m_old, mt_ref[buf, u])
        alpha = jnp.exp2(m_old - m_new)
        pr = jnp.exp2(s_ref[buf, u] - m_new).astype(BF16)
        m_ref[u] = m_new
        for hh in range(HEADS_PER_STEP):
            row0 = u * blk + hh * hd
            vx = jnp.concatenate([vt_ref[0, j, row0:row0 + hd, :], ones], axis=0)
            pv = _dot(vx, pr[:, hh * tq:(hh + 1) * tq])
            acc_ref[u, hh] = alpha[:, hh * tq:(hh + 1) * tq] * acc_ref[u, hh] + pv

    def all_streams(fn, *args, **kwargs):
        for u in range(n_streams):
            fn(u, *args, **kwargs)

    def step(t_next, buf_next, t_cur, buf_cur, masked=False):
        for u in range(n_streams):
            qk_stage(u, t_next, buf_next, masked=masked)
            pv_stage(u, t_cur, buf_cur)

    inner = (i > 0) & (i < n_q)
    odd = i % 2 == 1

    @pl.when(i == 0)
    def _():
        build_rhs()
        reset_state()
        all_streams(qk_stage, 0, diag, masked=True)

    def start(buf_first):
        build_rhs()
        step(0, buf_first, i - 1, diag)
        finish()
        reset_state()

    @pl.when(inner & odd)
    def _():
        start(1)

    @pl.when(inner & jnp.logical_not(odd))
    def _():
        start(0)
        step(1, 1, 0, 0)

    t_start = 1 - i % 2

    def body(jj, carry):
        t = t_start + 2 * jj
        step(t + 1, 0, t, 1)
        step(t + 2, 1, t + 1, 0)
        return carry

    lax.fori_loop(0, jnp.where(inner, (i - 1) // 2, 0), body, 0)

    @pl.when(inner)
    def _():
        step(i, diag, i - 1, 1, masked=True)

    @pl.when(i == n_q)
    def _():
        all_streams(pv_stage, n_q - 1, diag)
        finish()


def _attention(qt, k, aug, vt, *, tq, n_heads, n_streams):
    bsz, d, s = qt.shape
    nk, tk = vt.shape[1], vt.shape[3]
    assert tq == tk, "the causal mask is applied to the diagonal tile only"
    blk = HEADS_PER_STEP * HEAD_DIM
    wide = n_streams * blk
    lanes = aug.shape[2]
    n_q = s // tq
    return pl.pallas_call(
        functools.partial(_attn_kernel, tq=tq, tk=tk, n_heads=n_heads, n_streams=n_streams, n_q=n_q),
        grid=(bsz, d // wide, n_q + 1),
        in_specs=[
            pl.BlockSpec((1, wide, tq), lambda b, g, i: (b, g, jnp.minimum(i, n_q - 1))),
            pl.BlockSpec((1, s, wide), lambda b, g, i: (b, 0, g)),
            pl.BlockSpec((1, s, lanes), lambda b, g, i: (b, 0, 0)),
            pl.BlockSpec((1, nk, wide, tk), lambda b, g, i: (b, 0, g, 0)),
        ],
        out_specs=pl.BlockSpec((1, tq, wide), lambda b, g, i: (b, jnp.maximum(i - 1, 0), g)),
        out_shape=jax.ShapeDtypeStruct((bsz, s, d), BF16),
        scratch_shapes=[
            pltpu.VMEM((n_streams, blk + lanes, HEADS_PER_STEP * tq), BF16),
            pltpu.VMEM((3, n_streams, tk, HEADS_PER_STEP * tq), F32),
            pltpu.VMEM((3, n_streams, 1, HEADS_PER_STEP * tq), F32),
            pltpu.VMEM((n_streams, 1, HEADS_PER_STEP * tq), F32),
            pltpu.VMEM((n_streams, HEADS_PER_STEP, HEAD_DIM + SUM_ROWS, tq), F32),
        ],
        compiler_params=_params(("parallel", "parallel", "arbitrary"), 58),
        name="attention",
    )(qt, k, aug, vt)


def kernel(x, c, w_ada, b_ada, conv_w_in, conv_w, conv_w_out, w_ada_kv, b_ada_kv, w_k, w_v, w_f, b_f,
           attn_w_q, attn_w_o, ffn_w_gate, ffn_w_up, ffn_w_down, ln1_g, ln1_b, ln2_g, ln2_b):
    depth = w_ada.shape[0]
    n_a = conv_w_in.shape[0]
    d = x.shape[-1]
    n_heads = w_f.shape[1]
    assert d // n_heads == HEAD_DIM and 3 * n_heads <= V7X_LANES
    alpha = (2.0 * depth) ** 0.25
    tm = ROW_CHUNK
    tm_tail = 2 * ROW_CHUNK

    mods, modkv = _mods(c, w_ada, b_ada, w_ada_kv, b_ada_kv)
    bf = lambda w: w.astype(BF16)
    w_f_pad = jnp.pad(w_f, ((0, 0), (0, V7X_LANES - n_heads))).astype(BF16)
    b_f_pad = jnp.pad(b_f, (0, V7X_LANES - n_heads)).reshape(1, V7X_LANES)

    kv = None
    for l in range(depth):
        ln = jnp.stack([ln1_g[l], ln1_b[l], ln2_g[l], ln2_b[l]])
        if l < n_a:
            z = _conv_front(x, mods[l], bf(conv_w_in[l]), conv_w[l], tm=tm_tail)
            w_proj = conv_w_out[l]
        else:
            j = l - n_a
            if kv is None:
                qt, k, vt, aug = _qkv(x, mods[l], modkv, bf(w_k), bf(w_v.T), bf(attn_w_q[j].T), w_f_pad, b_f_pad,
                                      tm=tm_tail, tk=tm, n_heads=n_heads)
                kv = (k, aug, vt)
            else:
                raise NotImplementedError("more than one attention layer needs a q-only projection kernel")
            z = _attention(qt, *kv, tq=tm, n_heads=n_heads, n_streams=8)
            w_proj = attn_w_o[j]
        x = _tail(x, z, mods[l], bf(w_proj), bf(ffn_w_gate[l]), bf(ffn_w_up[l]), bf(ffn_w_down[l]), ln,
                  alpha=alpha, tm=tm_tail)
    return x
```

```python
import functools

import jax
import jax.numpy as jnp
from jax import lax
from jax.experimental import pallas as pl
from jax.experimental.pallas import tpu as pltpu

F32 = jnp.float32
BF16 = jnp.bfloat16

LN_EPS = 1e-5
HEAD_DIM = 64
N_MOD = 6
V7X_LANES = 128
V7X_VMEM_BYTES = 64 * 2**20
NEG_BIG = -1e30
LOG2_E = 1.4426950408889634
HEADS_PER_STEP = 2
SUM_ROWS = 16
ROW_CHUNK = 256


def _dot(a, b):
    return jnp.dot(a, b, preferred_element_type=F32)


def _dot_nt(a, b):
    return lax.dot_general(a, b, (((1,), (1,)), ((), ())), preferred_element_type=F32)


def _split3(x):
    hi = x.astype(BF16)
    r = x - hi.astype(F32)
    mid = r.astype(BF16)
    lo = (r - mid.astype(F32)).astype(BF16)
    return hi, mid, lo


def _layer_norm(x, g, b):
    mu = jnp.mean(x, axis=-1, keepdims=True)
    xc = x - mu
    var = jnp.mean(xc * xc, axis=-1, keepdims=True)
    return xc * lax.rsqrt(var + LN_EPS) * g + b


def _silu(x):
    return x / (1.0 + jnp.exp(-x))


def _const_spec(shape):
    nd = len(shape)
    return pl.BlockSpec(shape, lambda *_: (0,) * nd, pipeline_mode=pl.Buffered(1))


def _params(sem, vmem_mb):
    return pltpu.CompilerParams(dimension_semantics=sem, vmem_limit_bytes=vmem_mb * 2**20)


def _mods_kernel(c_ref, wada_ref, wkv_ref, b_ref, o_ref, *, n_ada):
    n = pl.program_id(0)
    ca = _silu(c_ref[...])
    ca_hi = ca.astype(BF16)
    ca_lo = (ca - ca_hi.astype(F32)).astype(BF16)

    def compute(w):
        w_hi = w.astype(BF16)
        w_lo = (w - w_hi.astype(F32)).astype(BF16)
        return _dot(ca_hi, w_hi) + _dot(ca_lo, w_hi) + _dot(ca_hi, w_lo)

    @pl.when(n < n_ada)
    def _():
        o_ref[0] = compute(wada_ref[0]) + b_ref[0]

    @pl.when(n >= n_ada)
    def _():
        o_ref[0] = compute(wkv_ref[...]) + b_ref[0]


def _mods(c, w_ada, b_ada, w_ada_kv, b_ada_kv):
    bsz, d = c.shape
    depth = w_ada.shape[0]
    n_ada = depth * N_MOD
    n_tot = n_ada + 2
    b_all = jnp.concatenate([b_ada.reshape(n_ada, d), b_ada_kv.reshape(2, d)], axis=0).reshape(n_tot, 1, d)
    out = pl.pallas_call(
        functools.partial(_mods_kernel, n_ada=n_ada),
        grid=(n_tot,),
        in_specs=[
            pl.BlockSpec((bsz, d), lambda n: (0, 0)),
            pl.BlockSpec((1, d, d), lambda n: (jnp.minimum(n, n_ada - 1) // N_MOD, 0, jnp.minimum(n, n_ada - 1) % N_MOD)),
            pl.BlockSpec((d, d), lambda n: (0, jnp.maximum(n - n_ada, 0))),
            pl.BlockSpec((1, 1, d), lambda n: (n, 0, 0)),
        ],
        out_specs=pl.BlockSpec((1, bsz, d), lambda n: (n, 0, 0)),
        out_shape=jax.ShapeDtypeStruct((n_tot, bsz, d), F32),
        compiler_params=_params(("arbitrary",), 40),
        name="mods",
    )(c, w_ada, w_ada_kv, b_all)
    mods = jnp.transpose(out, (1, 0, 2))
    per_layer = [mods[:, l * N_MOD:(l + 1) * N_MOD] for l in range(depth)]
    return per_layer, mods[:, n_ada:]


def _conv_kernel(x_ref, mod_ref, win_ref, cw_ref, z_ref, carry_ref, *, tm, d):
    @pl.when(pl.program_id(1) == 0)
    def _():
        carry_ref[...] = jnp.zeros_like(carry_ref)

    mod = mod_ref[0]
    h = (x_ref[0] * (1.0 + mod[1:2]) + mod[0:1]).astype(BF16)
    gc = _dot(h, win_ref[:, d:2 * d])
    v = _dot(h, win_ref[:, 2 * d:3 * d])
    u = gc * v
    row = lax.broadcasted_iota(jnp.int32, (tm, 1), 0)
    c_last = carry_ref[7:8, :]
    c_prev = carry_ref[6:7, :]
    um1 = jnp.where(row == 0, c_last, pltpu.roll(u, 1, axis=0))
    um2 = jnp.where(row == 0, c_prev, jnp.where(row == 1, c_last, pltpu.roll(u, 2, axis=0)))
    cw = cw_ref[...]
    conv = cw[0:1] * um2 + cw[1:2] * um1 + cw[2:3] * u
    carry_ref[...] = u[tm - 8:tm, :]
    gb = _dot(h, win_ref[:, 0:d])
    z_ref[0] = (gb * conv).astype(BF16)


def _conv_front(x, mod, w_in, conv_w, *, tm):
    bsz, s, d = x.shape
    return pl.pallas_call(
        functools.partial(_conv_kernel, tm=tm, d=d),
        grid=(bsz, s // tm),
        in_specs=[
            pl.BlockSpec((1, tm, d), lambda b, t: (b, t, 0)),
            pl.BlockSpec((1, N_MOD, d), lambda b, t: (b, 0, 0)),
            _const_spec(w_in.shape),
            _const_spec(conv_w.shape),
        ],
        out_specs=pl.BlockSpec((1, tm, d), lambda b, t: (b, t, 0)),
        out_shape=jax.ShapeDtypeStruct((bsz, s, d), BF16),
        scratch_shapes=[pltpu.VMEM((8, d), F32)],
        compiler_params=_params(("parallel", "arbitrary"), 48),
        name="conv_front",
    )(x, mod, w_in, conv_w)


def _tail_kernel(x_ref, z_ref, mod_ref, wp_ref, wg_ref, wu_ref, wd_ref, ln_ref, o_ref, *, alpha, tm):
    mod = mod_ref[0]
    ln = ln_ref[...]
    rows = [pl.ds(r, ROW_CHUNK) for r in range(0, tm, ROW_CHUNK)]
    ys = [_dot(z_ref[0, r, :], wp_ref[...]) for r in rows]
    x1s = [_layer_norm(alpha * x_ref[0, r, :] + mod[2:3] * y, ln[0:1], ln[1:2]) for r, y in zip(rows, ys)]
    hs = [(x1 * (1.0 + mod[4:5]) + mod[3:4]).astype(BF16) for x1 in x1s]
    gus = [(_dot(h, wg_ref[...]), _dot(h, wu_ref[...])) for h in hs]
    acts = [(_silu(g) * u).astype(BF16) for g, u in gus]
    y2s = [_dot(a, wd_ref[...]) for a in acts]
    for r, x1, y2 in zip(rows, x1s, y2s):
        o_ref[0, r, :] = _layer_norm(alpha * x1 + mod[5:6] * y2, ln[2:3], ln[3:4])


def _tail(x, z, mod, w_proj, w_gate, w_up, w_down, ln, *, alpha, tm):
    bsz, s, d = x.shape
    return pl.pallas_call(
        functools.partial(_tail_kernel, alpha=alpha, tm=tm),
        grid=(bsz, s // tm),
        in_specs=[
            pl.BlockSpec((1, tm, d), lambda b, t: (b, t, 0)),
            pl.BlockSpec((1, tm, d), lambda b, t: (b, t, 0)),
            pl.BlockSpec((1, N_MOD, d), lambda b, t: (b, 0, 0)),
            _const_spec(w_proj.shape),
            _const_spec(w_gate.shape),
            _const_spec(w_up.shape),
            _const_spec(w_down.shape),
            _const_spec(ln.shape),
        ],
        out_specs=pl.BlockSpec((1, tm, d), lambda b, t: (b, t, 0)),
        out_shape=jax.ShapeDtypeStruct((bsz, s, d), F32),
        compiler_params=_params(("parallel", "parallel"), 60),
        name="tail",
    )(x, z, mod, w_proj, w_gate, w_up, w_down, ln)


def _qkv_kernel(x_ref, mod_ref, modkv_ref, wk_ref, wvt_ref, wqt_ref, wf_ref, bf_ref,
                qt_ref, k_ref, vt_ref, aug_ref, carry_ref, *, tm, tk, n_heads, q_scale):
    @pl.when(pl.program_id(1) == 0)
    def _():
        carry_ref[...] = jnp.zeros_like(carry_ref)

    x = x_ref[0]
    mod = mod_ref[0]
    modkv = modkv_ref[0]
    hkv = (x * (1.0 + modkv[1:2]) + modkv[0:1]).astype(BF16)
    hq = (x * (1.0 + mod[1:2]) + mod[0:1]).astype(BF16)
    zf = _dot(hkv, wf_ref[...]) + bf_ref[...]
    k_ref[0] = _dot(hkv, wk_ref[...]).astype(BF16)
    vt = _dot_nt(wvt_ref[...], hkv).astype(BF16)
    for c in range(tm // tk):
        vt_ref[0, c] = vt[:, c * tk:(c + 1) * tk]
    qt_ref[0] = (_dot_nt(wqt_ref[...], hq) * q_scale).astype(BF16)

    lf = jnp.minimum(zf, 0.0) - jnp.log1p(jnp.exp(-jnp.abs(zf)))
    lane = lax.broadcasted_iota(jnp.int32, lf.shape, 1)
    lf = jnp.where(lane < n_heads, lf, 0.0)
    tri = jnp.where(lax.broadcasted_iota(jnp.int32, (ROW_CHUNK, ROW_CHUNK), 0)
                    >= lax.broadcasted_iota(jnp.int32, (ROW_CHUNK, ROW_CHUNK), 1), 1.0, 0.0).astype(BF16)
    total = carry_ref[0:1, :]
    for r in range(0, tm, ROW_CHUNK):
        hi, mid, lo = _split3(lf[r:r + ROW_CHUNK])
        fcum = _dot(tri, hi) + _dot(tri, mid) + _dot(tri, lo) + total
        total = fcum[ROW_CHUNK - 1:ROW_CHUNK, :]
        a_hi, a_mid, a_lo = _split3(-LOG2_E * fcum)
        aug = (a_hi.astype(F32) + pltpu.roll(a_mid.astype(F32), n_heads, axis=1)
               + pltpu.roll(a_lo.astype(F32), 2 * n_heads, axis=1))
        aug_ref[0, r:r + ROW_CHUNK, :] = aug.astype(BF16)
    carry_ref[0:1, :] = total


def _qkv(x, mod, modkv, w_k, w_vt, w_qt, w_f, b_f, *, tm, tk, n_heads):
    bsz, s, d = x.shape
    nt = s // tm
    lanes = w_f.shape[1]
    return pl.pallas_call(
        functools.partial(_qkv_kernel, tm=tm, tk=tk, n_heads=n_heads, q_scale=LOG2_E * HEAD_DIM ** -0.5),
        grid=(bsz, nt),
        in_specs=[
            pl.BlockSpec((1, tm, d), lambda b, t: (b, t, 0)),
            pl.BlockSpec((1, N_MOD, d), lambda b, t: (b, 0, 0)),
            pl.BlockSpec((1, 2, d), lambda b, t: (b, 0, 0)),
            _const_spec(w_k.shape),
            _const_spec(w_vt.shape),
            _const_spec(w_qt.shape),
            _const_spec(w_f.shape),
            _const_spec(b_f.shape),
        ],
        out_specs=[
            pl.BlockSpec((1, d, tm), lambda b, t: (b, 0, t)),
            pl.BlockSpec((1, tm, d), lambda b, t: (b, t, 0)),
            pl.BlockSpec((1, tm // tk, d, tk), lambda b, t: (b, t, 0, 0)),
            pl.BlockSpec((1, tm, lanes), lambda b, t: (b, t, 0)),
        ],
        out_shape=[
            jax.ShapeDtypeStruct((bsz, d, s), BF16),
            jax.ShapeDtypeStruct((bsz, s, d), BF16),
            jax.ShapeDtypeStruct((bsz, s // tk, d, tk), BF16),
            jax.ShapeDtypeStruct((bsz, s, lanes), BF16),
        ],
        scratch_shapes=[pltpu.VMEM((8, lanes), F32)],
        compiler_params=_params(("parallel", "arbitrary"), 48),
        name="qkv",
    )(x, mod, modkv, w_k, w_vt, w_qt, w_f, b_f)


def _attn_kernel(qt_ref, k_ref, aug_ref, vt_ref, vtd_ref, o_ref, rhs_ref, s_ref, mt_ref, m_ref, acc_ref,
                 *, tq, tk, n_heads, n_streams, n_q):
    g = pl.program_id(1)
    i = pl.program_id(2)
    hd = HEAD_DIM
    blk = HEADS_PER_STEP * hd
    diag = 2
    ones = jnp.ones((SUM_ROWS, tk), BF16)

    def build_rhs():
        r = lax.broadcasted_iota(jnp.int32, (blk, tq), 0)
        for u in range(n_streams):
            qt = qt_ref[0, u * blk:(u + 1) * blk, :].astype(F32)
            for hh in range(HEADS_PER_STEP):
                head = (g * n_streams + u) * HEADS_PER_STEP + hh
                sel = (r == head) | (r == head + n_heads) | (r == head + 2 * n_heads)
                rhs_ref[u, 0:blk, hh * tq:(hh + 1) * tq] = jnp.where(
                    (r >= hh * hd) & (r < (hh + 1) * hd), qt, 0.0).astype(BF16)
                rhs_ref[u, blk:2 * blk, hh * tq:(hh + 1) * tq] = jnp.where(sel, 1.0, 0.0).astype(BF16)

    def reset_state():
        m_ref[...] = jnp.full_like(m_ref, NEG_BIG)
        acc_ref[...] = jnp.zeros_like(acc_ref)

    def finish():
        outs = []
        for u in range(n_streams):
            for hh in range(HEADS_PER_STEP):
                acc = acc_ref[u, hh]
                outs.append(acc[0:hd] / acc[hd:hd + 1])
        o_ref[0] = jnp.concatenate(outs, axis=0).T.astype(BF16)

    def qk_stage(u, j, buf, masked=False):
        ks = pl.multiple_of(j * tk, tk)
        kx = jnp.concatenate([k_ref[0, pl.ds(ks, tk), u * blk:(u + 1) * blk], aug_ref[0, pl.ds(ks, tk), :]], axis=1)
        s = _dot(kx, rhs_ref[u])
        if masked:
            krow = lax.broadcasted_iota(jnp.int32, s.shape, 0)
            qcol = lax.broadcasted_iota(jnp.int32, s.shape, 1)
            qcol = jnp.where(qcol >= tq, qcol - tq, qcol)
            s = jnp.where(krow <= qcol, s, NEG_BIG)
        s_ref[buf, u] = s
        mt_ref[buf, u] = jnp.max(s, axis=0, keepdims=True)

    def pv_stage(u, j, buf, pending=False):
        m_old = m_ref[u]
        m_new = jnp.maximum(m_old, mt_ref[buf, u])
        alpha = jnp.exp2(m_old - m_new)
        pr = jnp.exp2(s_ref[buf, u] - m_new).astype(BF16)
        m_ref[u] = m_new
        for hh in range(HEADS_PER_STEP):
            row0 = u * blk + hh * hd
            vt_h = vtd_ref[0, 0, row0:row0 + hd, :] if pending else vt_ref[0, j, row0:row0 + hd, :]
            vx = jnp.concatenate([vt_h, ones], axis=0)
            pv = _dot(vx, pr[:, hh * tq:(hh + 1) * tq])
            acc_ref[u, hh] = alpha[:, hh * tq:(hh + 1) * tq] * acc_ref[u, hh] + pv

    def all_streams(fn, *args, **kwargs):
        for u in range(n_streams):
            fn(u, *args, **kwargs)

    def step(t_next, buf_next, t_cur, buf_cur, masked=False, pending=False):
        for u in range(n_streams):
            qk_stage(u, t_next, buf_next, masked=masked)
            pv_stage(u, t_cur, buf_cur, pending=pending)

    inner = (i > 0) & (i < n_q)
    odd = i % 2 == 1

    @pl.when(i == 0)
    def _():
        build_rhs()
        reset_state()
        all_streams(qk_stage, 0, diag, masked=True)

    def start(buf_first):
        build_rhs()
        step(0, buf_first, i - 1, diag, pending=True)
        finish()
        reset_state()

    @pl.when(inner & odd)
    def _():
        start(1)

    @pl.when(inner & jnp.logical_not(odd))
    def _():
        start(0)
        step(1, 1, 0, 0)

    t_start = 1 - i % 2

    def two_steps(t):
        step(t + 1, 0, t, 1)
        step(t + 2, 1, t + 1, 0)

    def body(jj, carry):
        t = t_start + 4 * jj
        two_steps(t)
        two_steps(t + 2)
        return carry

    n_pairs = jnp.where(inner, (i - 1) // 2, 0)
    lax.fori_loop(0, n_pairs // 2, body, 0)

    @pl.when(n_pairs % 2 == 1)
    def _():
        two_steps(t_start + 2 * (n_pairs - 1))

    @pl.when(inner)
    def _():
        step(i, diag, i - 1, 1, masked=True)

    @pl.when(i == n_q)
    def _():
        all_streams(pv_stage, n_q - 1, diag, pending=True)
        finish()


def _attention(qt, k, aug, vt, *, tq, n_heads, n_streams):
    bsz, d, s = qt.shape
    nk, tk = vt.shape[1], vt.shape[3]
    assert tq == tk, "the causal mask is applied to the diagonal tile only"
    blk = HEADS_PER_STEP * HEAD_DIM
    wide = n_streams * blk
    lanes = aug.shape[2]
    n_q = s // tq
    n_g = d // wide

    def resident(b, g, i):
        last = i == n_q
        wrap = g + 1 == n_g
        b_next = jnp.where(wrap, jnp.minimum(b + 1, bsz - 1), b)
        g_next = jnp.where(wrap, 0, g + 1)
        return jnp.where(last, b_next, b), jnp.where(last, g_next, g)

    def k_map(b, g, i):
        bb, gg = resident(b, g, i)
        return bb, 0, gg

    def aug_map(b, g, i):
        return resident(b, g, i)[0], 0, 0

    def vt_map(b, g, i):
        bb, gg = resident(b, g, i)
        return bb, 0, gg, 0

    return pl.pallas_call(
        functools.partial(_attn_kernel, tq=tq, tk=tk, n_heads=n_heads, n_streams=n_streams, n_q=n_q),
        grid=(bsz, d // wide, n_q + 1),
        in_specs=[
            pl.BlockSpec((1, wide, tq), lambda b, g, i: (b, g, jnp.minimum(i, n_q - 1))),
            pl.BlockSpec((1, s, wide), k_map),
            pl.BlockSpec((1, s, lanes), aug_map),
            pl.BlockSpec((1, nk, wide, tk), vt_map),
            pl.BlockSpec((1, 1, wide, tk), lambda b, g, i: (b, jnp.maximum(i - 1, 0), g, 0)),
        ],
        out_specs=pl.BlockSpec((1, tq, wide), lambda b, g, i: (b, jnp.maximum(i - 1, 0), g)),
        out_shape=jax.ShapeDtypeStruct((bsz, s, d), BF16),
        scratch_shapes=[
            pltpu.VMEM((n_streams, blk + lanes, HEADS_PER_STEP * tq), BF16),
            pltpu.VMEM((3, n_streams, tk, HEADS_PER_STEP * tq), F32),
            pltpu.VMEM((3, n_streams, 1, HEADS_PER_STEP * tq), F32),
            pltpu.VMEM((n_streams, 1, HEADS_PER_STEP * tq), F32),
            pltpu.VMEM((n_streams, HEADS_PER_STEP, HEAD_DIM + SUM_ROWS, tq), F32),
        ],
        compiler_params=_params(("parallel", "parallel", "arbitrary"), 58),
        name="attention",
    )(qt, k, aug, vt, vt)


def kernel(x, c, w_ada, b_ada, conv_w_in, conv_w, conv_w_out, w_ada_kv, b_ada_kv, w_k, w_v, w_f, b_f,
           attn_w_q, attn_w_o, ffn_w_gate, ffn_w_up, ffn_w_down, ln1_g, ln1_b, ln2_g, ln2_b):
    depth = w_ada.shape[0]
    n_a = conv_w_in.shape[0]
    d = x.shape[-1]
    n_heads = w_f.shape[1]
    assert d // n_heads == HEAD_DIM and 3 * n_heads <= V7X_LANES
    alpha = (2.0 * depth) ** 0.25
    tm = ROW_CHUNK
    tm_tail = 2 * ROW_CHUNK

    mods, modkv = _mods(c, w_ada, b_ada, w_ada_kv, b_ada_kv)
    bf = lambda w: w.astype(BF16)
    w_f_pad = jnp.pad(w_f, ((0, 0), (0, V7X_LANES - n_heads))).astype(BF16)
    b_f_pad = jnp.pad(b_f, (0, V7X_LANES - n_heads)).reshape(1, V7X_LANES)

    kv = None
    for l in range(depth):
        ln = jnp.stack([ln1_g[l], ln1_b[l], ln2_g[l], ln2_b[l]])
        if l < n_a:
            z = _conv_front(x, mods[l], bf(conv_w_in[l]), conv_w[l], tm=tm_tail)
            w_proj = conv_w_out[l]
        else:
            j = l - n_a
            if kv is None:
                qt, k, vt, aug = _qkv(x, mods[l], modkv, bf(w_k), bf(w_v.T), bf(attn_w_q[j].T), w_f_pad, b_f_pad,
                                      tm=tm_tail, tk=tm, n_heads=n_heads)
                kv = (k, aug, vt)
            else:
                raise NotImplementedError("more than one attention layer needs a q-only projection kernel")
            z = _attention(qt, *kv, tq=tm, n_heads=n_heads, n_streams=8)
            w_proj = attn_w_o[j]
        x = _tail(x, z, mods[l], bf(w_proj), bf(ffn_w_gate[l]), bf(ffn_w_up[l]), bf(ffn_w_down[l]), ln,
                  alpha=alpha, tm=tm_tail)
    return x
```

```python
import functools

import jax
import jax.numpy as jnp
from jax import lax
from jax.experimental import pallas as pl
from jax.experimental.pallas import tpu as pltpu

F32 = jnp.float32
BF16 = jnp.bfloat16

LN_EPS = 1e-5
HEAD_DIM = 64
N_MOD = 6
V7X_LANES = 128
V7X_VMEM_BYTES = 64 * 2**20
NEG_BIG = -1e30
LOG2_E = 1.4426950408889634
HEADS_PER_STEP = 2
SUM_ROWS = 16
ROW_CHUNK = 256


def _dot(a, b):
    return jnp.dot(a, b, preferred_element_type=F32)


def _dot_nt(a, b):
    return lax.dot_general(a, b, (((1,), (1,)), ((), ())), preferred_element_type=F32)


def _split3(x):
    hi = x.astype(BF16)
    r = x - hi.astype(F32)
    mid = r.astype(BF16)
    lo = (r - mid.astype(F32)).astype(BF16)
    return hi, mid, lo


def _layer_norm(x, g, b):
    mu = jnp.mean(x, axis=-1, keepdims=True)
    xc = x - mu
    var = jnp.mean(xc * xc, axis=-1, keepdims=True)
    return xc * lax.rsqrt(var + LN_EPS) * g + b


def _silu(x):
    return x / (1.0 + jnp.exp(-x))


def _const_spec(shape):
    nd = len(shape)
    return pl.BlockSpec(shape, lambda *_: (0,) * nd, pipeline_mode=pl.Buffered(1))


def _params(sem, vmem_mb):
    return pltpu.CompilerParams(dimension_semantics=sem, vmem_limit_bytes=vmem_mb * 2**20)


def _mods_kernel(c_ref, wada_ref, wkv_ref, b_ref, o_ref, *, n_ada):
    n = pl.program_id(0)
    ca = _silu(c_ref[...])
    ca_hi = ca.astype(BF16)
    ca_lo = (ca - ca_hi.astype(F32)).astype(BF16)

    def compute(w):
        w_hi = w.astype(BF16)
        w_lo = (w - w_hi.astype(F32)).astype(BF16)
        return _dot(ca_hi, w_hi) + _dot(ca_lo, w_hi) + _dot(ca_hi, w_lo)

    @pl.when(n < n_ada)
    def _():
        o_ref[0] = compute(wada_ref[0]) + b_ref[0]

    @pl.when(n >= n_ada)
    def _():
        o_ref[0] = compute(wkv_ref[...]) + b_ref[0]


def _mods(c, w_ada, b_ada, w_ada_kv, b_ada_kv):
    bsz, d = c.shape
    depth = w_ada.shape[0]
    n_ada = depth * N_MOD
    n_tot = n_ada + 2
    b_all = jnp.concatenate([b_ada.reshape(n_ada, d), b_ada_kv.reshape(2, d)], axis=0).reshape(n_tot, 1, d)
    out = pl.pallas_call(
        functools.partial(_mods_kernel, n_ada=n_ada),
        grid=(n_tot,),
        in_specs=[
            pl.BlockSpec((bsz, d), lambda n: (0, 0)),
            pl.BlockSpec((1, d, d), lambda n: (jnp.minimum(n, n_ada - 1) // N_MOD, 0, jnp.minimum(n, n_ada - 1) % N_MOD)),
            pl.BlockSpec((d, d), lambda n: (0, jnp.maximum(n - n_ada, 0))),
            pl.BlockSpec((1, 1, d), lambda n: (n, 0, 0)),
        ],
        out_specs=pl.BlockSpec((1, bsz, d), lambda n: (n, 0, 0)),
        out_shape=jax.ShapeDtypeStruct((n_tot, bsz, d), F32),
        compiler_params=_params(("arbitrary",), 40),
        name="mods",
    )(c, w_ada, w_ada_kv, b_all)
    mods = jnp.transpose(out, (1, 0, 2))
    per_layer = [mods[:, l * N_MOD:(l + 1) * N_MOD] for l in range(depth)]
    return per_layer, mods[:, n_ada:]


def _conv_kernel(x_ref, mod_ref, win_ref, cw_ref, z_ref, carry_ref, *, tm, d):
    @pl.when(pl.program_id(1) == 0)
    def _():
        carry_ref[...] = jnp.zeros_like(carry_ref)

    mod = mod_ref[0]
    h = (x_ref[0] * (1.0 + mod[1:2]) + mod[0:1]).astype(BF16)
    gc = _dot(h, win_ref[:, d:2 * d])
    v = _dot(h, win_ref[:, 2 * d:3 * d])
    u = gc * v
    row = lax.broadcasted_iota(jnp.int32, (tm, 1), 0)
    c_last = carry_ref[7:8, :]
    c_prev = carry_ref[6:7, :]
    um1 = jnp.where(row == 0, c_last, pltpu.roll(u, 1, axis=0))
    um2 = jnp.where(row == 0, c_prev, jnp.where(row == 1, c_last, pltpu.roll(u, 2, axis=0)))
    cw = cw_ref[...]
    conv = cw[0:1] * um2 + cw[1:2] * um1 + cw[2:3] * u
    carry_ref[...] = u[tm - 8:tm, :]
    gb = _dot(h, win_ref[:, 0:d])
    z_ref[0] = (gb * conv).astype(BF16)


def _conv_front(x, mod, w_in, conv_w, *, tm):
    bsz, s, d = x.shape
    return pl.pallas_call(
        functools.partial(_conv_kernel, tm=tm, d=d),
        grid=(bsz, s // tm),
        in_specs=[
            pl.BlockSpec((1, tm, d), lambda b, t: (b, t, 0)),
            pl.BlockSpec((1, N_MOD, d), lambda b, t: (b, 0, 0)),
            _const_spec(w_in.shape),
            _const_spec(conv_w.shape),
        ],
        out_specs=pl.BlockSpec((1, tm, d), lambda b, t: (b, t, 0)),
        out_shape=jax.ShapeDtypeStruct((bsz, s, d), BF16),
        scratch_shapes=[pltpu.VMEM((8, d), F32)],
        compiler_params=_params(("parallel", "arbitrary"), 48),
        name="conv_front",
    )(x, mod, w_in, conv_w)


def _tail_kernel(x_ref, z_ref, mod_ref, wp_ref, wg_ref, wu_ref, wd_ref, ln_ref, o_ref, done_ref, res_ref,
                 *, alpha, tm, n_t):
    t = pl.program_id(1)
    mod = mod_ref[0]
    ln = ln_ref[...]
    last = tm - ROW_CHUNK

    def write_previous():
        o_ref[0, 0:last, :] = done_ref[...]
        o_ref[0, last:tm, :] = _layer_norm(res_ref[...], ln[2:3], ln[3:4])

    def compute():
        starts = list(range(0, tm, ROW_CHUNK))
        rows = [pl.ds(r, ROW_CHUNK) for r in starts]
        ys = [_dot(z_ref[0, r, :], wp_ref[...]) for r in rows]
        x1s = [_layer_norm(alpha * x_ref[0, r, :] + mod[2:3] * y, ln[0:1], ln[1:2]) for r, y in zip(rows, ys)]
        hs = [(x1 * (1.0 + mod[4:5]) + mod[3:4]).astype(BF16) for x1 in x1s]
        gus = [(_dot(h, wg_ref[...]), _dot(h, wu_ref[...])) for h in hs]
        acts = [(_silu(g) * u).astype(BF16) for g, u in gus]
        y2s = [_dot(a, wd_ref[...]) for a in acts]
        for r0, r, x1, y2 in zip(starts, rows, x1s, y2s):
            res = alpha * x1 + mod[5:6] * y2
            if r0 < last:
                done_ref[r, :] = _layer_norm(res, ln[2:3], ln[3:4])
            else:
                res_ref[...] = res

    @pl.when(t == 0)
    def _():
        compute()

    @pl.when((t > 0) & (t < n_t))
    def _():
        write_previous()
        compute()

    @pl.when(t == n_t)
    def _():
        write_previous()


def _tail(x, z, mod, w_proj, w_gate, w_up, w_down, ln, *, alpha, tm):
    bsz, s, d = x.shape
    n_t = s // tm
    assert tm > ROW_CHUNK and tm % ROW_CHUNK == 0
    cur = lambda b, t: (b, jnp.minimum(t, n_t - 1), 0)
    return pl.pallas_call(
        functools.partial(_tail_kernel, alpha=alpha, tm=tm, n_t=n_t),
        grid=(bsz, n_t + 1),
        in_specs=[
            pl.BlockSpec((1, tm, d), cur),
            pl.BlockSpec((1, tm, d), cur),
            pl.BlockSpec((1, N_MOD, d), lambda b, t: (b, 0, 0)),
            _const_spec(w_proj.shape),
            _const_spec(w_gate.shape),
            _const_spec(w_up.shape),
            _const_spec(w_down.shape),
            _const_spec(ln.shape),
        ],
        out_specs=pl.BlockSpec((1, tm, d), lambda b, t: (b, jnp.maximum(t - 1, 0), 0)),
        out_shape=jax.ShapeDtypeStruct((bsz, s, d), F32),
        scratch_shapes=[
            pltpu.VMEM((tm - ROW_CHUNK, d), F32),
            pltpu.VMEM((ROW_CHUNK, d), F32),
        ],
        compiler_params=_params(("parallel", "arbitrary"), 60),
        name="tail",
    )(x, z, mod, w_proj, w_gate, w_up, w_down, ln)


def _qkv_kernel(x_ref, mod_ref, modkv_ref, wk_ref, wvt_ref, wqt_ref, wf_ref, bf_ref,
                qt_ref, k_ref, vt_ref, aug_ref, carry_ref, *, tm, tk, n_heads, q_scale):
    @pl.when(pl.program_id(1) == 0)
    def _():
        carry_ref[...] = jnp.zeros_like(carry_ref)

    x = x_ref[0]
    mod = mod_ref[0]
    modkv = modkv_ref[0]
    hkv = (x * (1.0 + modkv[1:2]) + modkv[0:1]).astype(BF16)
    hq = (x * (1.0 + mod[1:2]) + mod[0:1]).astype(BF16)
    zf = _dot(hkv, wf_ref[...]) + bf_ref[...]
    k_ref[0] = _dot(hkv, wk_ref[...]).astype(BF16)
    vt = _dot_nt(wvt_ref[...], hkv).astype(BF16)
    for c in range(tm // tk):
        vt_ref[0, c] = vt[:, c * tk:(c + 1) * tk]
    qt_ref[0] = (_dot_nt(wqt_ref[...], hq) * q_scale).astype(BF16)

    lf = jnp.minimum(zf, 0.0) - jnp.log1p(jnp.exp(-jnp.abs(zf)))
    lane = lax.broadcasted_iota(jnp.int32, lf.shape, 1)
    lf = jnp.where(lane < n_heads, lf, 0.0)
    tri = jnp.where(lax.broadcasted_iota(jnp.int32, (ROW_CHUNK, ROW_CHUNK), 0)
                    >= lax.broadcasted_iota(jnp.int32, (ROW_CHUNK, ROW_CHUNK), 1), 1.0, 0.0).astype(BF16)
    total = carry_ref[0:1, :]
    for r in range(0, tm, ROW_CHUNK):
        hi, mid, lo = _split3(lf[r:r + ROW_CHUNK])
        fcum = _dot(tri, hi) + _dot(tri, mid) + _dot(tri, lo) + total
        total = fcum[ROW_CHUNK - 1:ROW_CHUNK, :]
        a_hi, a_mid, a_lo = _split3(-LOG2_E * fcum)
        aug = (a_hi.astype(F32) + pltpu.roll(a_mid.astype(F32), n_heads, axis=1)
               + pltpu.roll(a_lo.astype(F32), 2 * n_heads, axis=1))
        aug_ref[0, r:r + ROW_CHUNK, :] = aug.astype(BF16)
    carry_ref[0:1, :] = total


def _qkv(x, mod, modkv, w_k, w_vt, w_qt, w_f, b_f, *, tm, tk, n_heads):
    bsz, s, d = x.shape
    nt = s // tm
    lanes = w_f.shape[1]
    return pl.pallas_call(
        functools.partial(_qkv_kernel, tm=tm, tk=tk, n_heads=n_heads, q_scale=LOG2_E * HEAD_DIM ** -0.5),
        grid=(bsz, nt),
        in_specs=[
            pl.BlockSpec((1, tm, d), lambda b, t: (b, t, 0)),
            pl.BlockSpec((1, N_MOD, d), lambda b, t: (b, 0, 0)),
            pl.BlockSpec((1, 2, d), lambda b, t: (b, 0, 0)),
            _const_spec(w_k.shape),
            _const_spec(w_vt.shape),
            _const_spec(w_qt.shape),
            _const_spec(w_f.shape),
            _const_spec(b_f.shape),
        ],
        out_specs=[
            pl.BlockSpec((1, d, tm), lambda b, t: (b, 0, t)),
            pl.BlockSpec((1, tm, d), lambda b, t: (b, t, 0)),
            pl.BlockSpec((1, tm // tk, d, tk), lambda b, t: (b, t, 0, 0)),
            pl.BlockSpec((1, tm, lanes), lambda b, t: (b, t, 0)),
        ],
        out_shape=[
            jax.ShapeDtypeStruct((bsz, d, s), BF16),
            jax.ShapeDtypeStruct((bsz, s, d), BF16),
            jax.ShapeDtypeStruct((bsz, s // tk, d, tk), BF16),
            jax.ShapeDtypeStruct((bsz, s, lanes), BF16),
        ],
        scratch_shapes=[pltpu.VMEM((8, lanes), F32)],
        compiler_params=_params(("parallel", "arbitrary"), 48),
        name="qkv",
    )(x, mod, modkv, w_k, w_vt, w_qt, w_f, b_f)


def _attn_kernel(qt_ref, k_ref, aug_ref, vt_ref, vtd_ref, o_ref, rhs_ref, s_ref, mt_ref, m_ref, acc_ref,
                 *, tq, tk, n_heads, n_streams, n_q):
    g = pl.program_id(1)
    i = pl.program_id(2)
    hd = HEAD_DIM
    blk = HEADS_PER_STEP * hd
    diag = 2
    ones = jnp.ones((SUM_ROWS, tk), BF16)

    def build_rhs():
        r = lax.broadcasted_iota(jnp.int32, (blk, tq), 0)
        for u in range(n_streams):
            qt = qt_ref[0, u * blk:(u + 1) * blk, :].astype(F32)
            for hh in range(HEADS_PER_STEP):
                head = (g * n_streams + u) * HEADS_PER_STEP + hh
                sel = (r == head) | (r == head + n_heads) | (r == head + 2 * n_heads)
                rhs_ref[u, 0:blk, hh * tq:(hh + 1) * tq] = jnp.where(
                    (r >= hh * hd) & (r < (hh + 1) * hd), qt, 0.0).astype(BF16)
                rhs_ref[u, blk:2 * blk, hh * tq:(hh + 1) * tq] = jnp.where(sel, 1.0, 0.0).astype(BF16)

    def reset_state():
        m_ref[...] = jnp.full_like(m_ref, NEG_BIG)
        acc_ref[...] = jnp.zeros_like(acc_ref)

    def finish():
        outs = []
        for u in range(n_streams):
            for hh in range(HEADS_PER_STEP):
                acc = acc_ref[u, hh]
                outs.append(acc[0:hd] / acc[hd:hd + 1])
        o_ref[0] = jnp.concatenate(outs, axis=0).T.astype(BF16)

    def qk_stage(u, j, buf, masked=False):
        ks = pl.multiple_of(j * tk, tk)
        kx = jnp.concatenate([k_ref[0, pl.ds(ks, tk), u * blk:(u + 1) * blk], aug_ref[0, pl.ds(ks, tk), :]], axis=1)
        s = _dot(kx, rhs_ref[u])
        if masked:
            krow = lax.broadcasted_iota(jnp.int32, s.shape, 0)
            qcol = lax.broadcasted_iota(jnp.int32, s.shape, 1)
            qcol = jnp.where(qcol >= tq, qcol - tq, qcol)
            s = jnp.where(krow <= qcol, s, NEG_BIG)
        s_ref[buf, u] = s
        mt_ref[buf, u] = jnp.max(s, axis=0, keepdims=True)

    def pv_stage(u, j, buf, pending=False):
        m_old = m_ref[u]
        m_new = jnp.maximum(m_old, mt_ref[buf, u])
        alpha = jnp.exp2(m_old - m_new)
        pr = jnp.exp2(s_ref[buf, u] - m_new).astype(BF16)
        m_ref[u] = m_new
        for hh in range(HEADS_PER_STEP):
            row0 = u * blk + hh * hd
            vt_h = vtd_ref[0, 0, row0:row0 + hd, :] if pending else vt_ref[0, j, row0:row0 + hd, :]
            vx = jnp.concatenate([vt_h, ones], axis=0)
            pv = _dot(vx, pr[:, hh * tq:(hh + 1) * tq])
            acc_ref[u, hh] = alpha[:, hh * tq:(hh + 1) * tq] * acc_ref[u, hh] + pv

    def all_streams(fn, *args, **kwargs):
        for u in range(n_streams):
            fn(u, *args, **kwargs)

    def step(t_next, buf_next, t_cur, buf_cur, masked=False, pending=False):
        for u in range(n_streams):
            qk_stage(u, t_next, buf_next, masked=masked)
            pv_stage(u, t_cur, buf_cur, pending=pending)

    inner = (i > 0) & (i < n_q)
    odd = i % 2 == 1

    @pl.when(i == 0)
    def _():
        build_rhs()
        reset_state()
        all_streams(qk_stage, 0, diag, masked=True)

    def start(buf_first):
        build_rhs()
        step(0, buf_first, i - 1, diag, pending=True)
        finish()
        reset_state()

    @pl.when(inner & odd)
    def _():
        start(1)

    @pl.when(inner & jnp.logical_not(odd))
    def _():
        start(0)
        step(1, 1, 0, 0)

    t_start = 1 - i % 2

    def two_steps(t):
        step(t + 1, 0, t, 1)
        step(t + 2, 1, t + 1, 0)

    def body(jj, carry):
        t = t_start + 4 * jj
        two_steps(t)
        two_steps(t + 2)
        return carry

    n_pairs = jnp.where(inner, (i - 1) // 2, 0)
    lax.fori_loop(0, n_pairs // 2, body, 0)

    @pl.when(n_pairs % 2 == 1)
    def _():
        two_steps(t_start + 2 * (n_pairs - 1))

    @pl.when(inner)
    def _():
        step(i, diag, i - 1, 1, masked=True)

    @pl.when(i == n_q)
    def _():
        all_streams(pv_stage, n_q - 1, diag, pending=True)
        finish()


def _attention(qt, k, aug, vt, *, tq, n_heads, n_streams):
    bsz, d, s = qt.shape
    nk, tk = vt.shape[1], vt.shape[3]
    assert tq == tk, "the causal mask is applied to the diagonal tile only"
    blk = HEADS_PER_STEP * HEAD_DIM
    wide = n_streams * blk
    lanes = aug.shape[2]
    n_q = s // tq
    n_g = d // wide

    def resident(b, g, i):
        last = i == n_q
        wrap = g + 1 == n_g
        b_next = jnp.where(wrap, jnp.minimum(b + 1, bsz - 1), b)
        g_next = jnp.where(wrap, 0, g + 1)
        return jnp.where(last, b_next, b), jnp.where(last, g_next, g)

    def k_map(b, g, i):
        bb, gg = resident(b, g, i)
        return bb, 0, gg

    def aug_map(b, g, i):
        return resident(b, g, i)[0], 0, 0

    def vt_map(b, g, i):
        bb, gg = resident(b, g, i)
        return bb, 0, gg, 0

    return pl.pallas_call(
        functools.partial(_attn_kernel, tq=tq, tk=tk, n_heads=n_heads, n_streams=n_streams, n_q=n_q),
        grid=(bsz, d // wide, n_q + 1),
        in_specs=[
            pl.BlockSpec((1, wide, tq), lambda b, g, i: (b, g, jnp.minimum(i, n_q - 1))),
            pl.BlockSpec((1, s, wide), k_map),
            pl.BlockSpec((1, s, lanes), aug_map),
            pl.BlockSpec((1, nk, wide, tk), vt_map),
            pl.BlockSpec((1, 1, wide, tk), lambda b, g, i: (b, jnp.maximum(i - 1, 0), g, 0)),
        ],
        out_specs=pl.BlockSpec((1, tq, wide), lambda b, g, i: (b, jnp.maximum(i - 1, 0), g)),
        out_shape=jax.ShapeDtypeStruct((bsz, s, d), BF16),
        scratch_shapes=[
            pltpu.VMEM((n_streams, blk + lanes, HEADS_PER_STEP * tq), BF16),
            pltpu.VMEM((3, n_streams, tk, HEADS_PER_STEP * tq), F32),
            pltpu.VMEM((3, n_streams, 1, HEADS_PER_STEP * tq), F32),
            pltpu.VMEM((n_streams, 1, HEADS_PER_STEP * tq), F32),
            pltpu.VMEM((n_streams, HEADS_PER_STEP, HEAD_DIM + SUM_ROWS, tq), F32),
        ],
        compiler_params=_params(("parallel", "parallel", "arbitrary"), 58),
        name="attention",
    )(qt, k, aug, vt, vt)


def kernel(x, c, w_ada, b_ada, conv_w_in, conv_w, conv_w_out, w_ada_kv, b_ada_kv, w_k, w_v, w_f, b_f,
           attn_w_q, attn_w_o, ffn_w_gate, ffn_w_up, ffn_w_down, ln1_g, ln1_b, ln2_g, ln2_b):
    depth = w_ada.shape[0]
    n_a = conv_w_in.shape[0]
    d = x.shape[-1]
    n_heads = w_f.shape[1]
    assert d // n_heads == HEAD_DIM and 3 * n_heads <= V7X_LANES
    alpha = (2.0 * depth) ** 0.25
    tm = ROW_CHUNK
    tm_tail = 2 * ROW_CHUNK
    tm_proj = 4 * ROW_CHUNK

    mods, modkv = _mods(c, w_ada, b_ada, w_ada_kv, b_ada_kv)
    bf = lambda w: w.astype(BF16)
    w_f_pad = jnp.pad(w_f, ((0, 0), (0, V7X_LANES - n_heads))).astype(BF16)
    b_f_pad = jnp.pad(b_f, (0, V7X_LANES - n_heads)).reshape(1, V7X_LANES)

    kv = None
    for l in range(depth):
        ln = jnp.stack([ln1_g[l], ln1_b[l], ln2_g[l], ln2_b[l]])
        if l < n_a:
            z = _conv_front(x, mods[l], bf(conv_w_in[l]), conv_w[l], tm=tm_proj)
            w_proj = conv_w_out[l]
        else:
            j = l - n_a
            if kv is None:
                qt, k, vt, aug = _qkv(x, mods[l], modkv, bf(w_k), bf(w_v.T), bf(attn_w_q[j].T), w_f_pad, b_f_pad,
                                      tm=tm_proj, tk=tm, n_heads=n_heads)
                kv = (k, aug, vt)
            else:
                raise NotImplementedError("more than one attention layer needs a q-only projection kernel")
            z = _attention(qt, *kv, tq=tm, n_heads=n_heads, n_streams=8)
            w_proj = attn_w_o[j]
        x = _tail(x, z, mods[l], bf(w_proj), bf(ffn_w_gate[l]), bf(ffn_w_up[l]), bf(ffn_w_down[l]), ln,
                  alpha=alpha, tm=tm_tail)
    return x
```

```python
import functools

import jax
import jax.numpy as jnp
from jax import lax
from jax.experimental import pallas as pl
from jax.experimental.pallas import tpu as pltpu

F32 = jnp.float32
BF16 = jnp.bfloat16

LN_EPS = 1e-5
HEAD_DIM = 64
N_MOD = 6
V7X_LANES = 128
V7X_VMEM_MIB = 64
VMEM_LIMIT_MIB = {"mods": 40, "conv_front": 48, "tail": 60, "qkv": 48, "attention": 58}
assert max(VMEM_LIMIT_MIB.values()) < V7X_VMEM_MIB
NEG_BIG = -1e30
LOG2_E = 1.4426950408889634
HEADS_PER_STEP = 2
SUM_ROWS = 16
ROW_CHUNK = 256


def _dot(a, b):
    return jnp.dot(a, b, preferred_element_type=F32)


def _dot_nt(a, b):
    return lax.dot_general(a, b, (((1,), (1,)), ((), ())), preferred_element_type=F32)


def _split3(x):
    hi = x.astype(BF16)
    r = x - hi.astype(F32)
    mid = r.astype(BF16)
    lo = (r - mid.astype(F32)).astype(BF16)
    return hi, mid, lo


def _layer_norm(x, g, b):
    mu = jnp.mean(x, axis=-1, keepdims=True)
    xc = x - mu
    var = jnp.mean(xc * xc, axis=-1, keepdims=True)
    return xc * lax.rsqrt(var + LN_EPS) * g + b


def _silu(x):
    return x / (1.0 + jnp.exp(-x))


def _const_spec(shape):
    nd = len(shape)
    return pl.BlockSpec(shape, lambda *_: (0,) * nd, pipeline_mode=pl.Buffered(1))


def _params(sem, name):
    return pltpu.CompilerParams(dimension_semantics=sem, vmem_limit_bytes=VMEM_LIMIT_MIB[name] * 2**20)


def _mods_kernel(c_ref, wada_ref, wkv_ref, b_ref, o_ref, *, n_ada):
    n = pl.program_id(0)
    ca = _silu(c_ref[...])
    ca_hi = ca.astype(BF16)
    ca_lo = (ca - ca_hi.astype(F32)).astype(BF16)

    def compute(w):
        w_hi = w.astype(BF16)
        w_lo = (w - w_hi.astype(F32)).astype(BF16)
        return _dot(ca_hi, w_hi) + _dot(ca_lo, w_hi) + _dot(ca_hi, w_lo)

    @pl.when(n < n_ada)
    def _():
        o_ref[0] = compute(wada_ref[0]) + b_ref[0]

    @pl.when(n >= n_ada)
    def _():
        o_ref[0] = compute(wkv_ref[...]) + b_ref[0]


def _mods(c, w_ada, b_ada, w_ada_kv, b_ada_kv):
    bsz, d = c.shape
    depth = w_ada.shape[0]
    n_ada = depth * N_MOD
    n_tot = n_ada + 2
    b_all = jnp.concatenate([b_ada.reshape(n_ada, d), b_ada_kv.reshape(2, d)], axis=0).reshape(n_tot, 1, d)
    out = pl.pallas_call(
        functools.partial(_mods_kernel, n_ada=n_ada),
        grid=(n_tot,),
        in_specs=[
            pl.BlockSpec((bsz, d), lambda n: (0, 0)),
            pl.BlockSpec((1, d, d), lambda n: (jnp.minimum(n, n_ada - 1) // N_MOD, 0, jnp.minimum(n, n_ada - 1) % N_MOD)),
            pl.BlockSpec((d, d), lambda n: (0, jnp.maximum(n - n_ada, 0))),
            pl.BlockSpec((1, 1, d), lambda n: (n, 0, 0)),
        ],
        out_specs=pl.BlockSpec((1, bsz, d), lambda n: (n, 0, 0)),
        out_shape=jax.ShapeDtypeStruct((n_tot, bsz, d), F32),
        compiler_params=_params(("arbitrary",), "mods"),
        name="mods",
    )(c, w_ada, w_ada_kv, b_all)
    mods = jnp.transpose(out, (1, 0, 2))
    per_layer = [mods[:, l * N_MOD:(l + 1) * N_MOD] for l in range(depth)]
    return per_layer, mods[:, n_ada:]


def _conv_kernel(x_ref, mod_ref, win_ref, cw_ref, z_ref, carry_ref, *, tm, d):
    @pl.when(pl.program_id(1) == 0)
    def _():
        carry_ref[...] = jnp.zeros_like(carry_ref)

    mod = mod_ref[0]
    h = (x_ref[0] * (1.0 + mod[1:2]) + mod[0:1]).astype(BF16)
    gc = _dot(h, win_ref[:, d:2 * d])
    v = _dot(h, win_ref[:, 2 * d:3 * d])
    u = gc * v
    row = lax.broadcasted_iota(jnp.int32, (tm, 1), 0)
    c_last = carry_ref[7:8, :]
    c_prev = carry_ref[6:7, :]
    um1 = jnp.where(row == 0, c_last, pltpu.roll(u, 1, axis=0))
    um2 = jnp.where(row == 0, c_prev, jnp.where(row == 1, c_last, pltpu.roll(u, 2, axis=0)))
    cw = cw_ref[...]
    conv = cw[0:1] * um2 + cw[1:2] * um1 + cw[2:3] * u
    carry_ref[...] = u[tm - 8:tm, :]
    gb = _dot(h, win_ref[:, 0:d])
    z_ref[0] = (gb * conv).astype(BF16)


def _conv_front(x, mod, w_in, conv_w, *, tm):
    bsz, s, d = x.shape
    return pl.pallas_call(
        functools.partial(_conv_kernel, tm=tm, d=d),
        grid=(bsz, s // tm),
        in_specs=[
            pl.BlockSpec((1, tm, d), lambda b, t: (b, t, 0)),
            pl.BlockSpec((1, N_MOD, d), lambda b, t: (b, 0, 0)),
            _const_spec(w_in.shape),
            _const_spec(conv_w.shape),
        ],
        out_specs=pl.BlockSpec((1, tm, d), lambda b, t: (b, t, 0)),
        out_shape=jax.ShapeDtypeStruct((bsz, s, d), BF16),
        scratch_shapes=[pltpu.VMEM((8, d), F32)],
        compiler_params=_params(("parallel", "arbitrary"), "conv_front"),
        name="conv_front",
    )(x, mod, w_in, conv_w)


def _tail_kernel(x_ref, z_ref, mod_ref, wp_ref, wg_ref, wu_ref, wd_ref, ln_ref, o_ref, *, alpha, tm):
    mod = mod_ref[0]
    ln = ln_ref[...]
    rows = [pl.ds(r, ROW_CHUNK) for r in range(0, tm, ROW_CHUNK)]
    ys = [_dot(z_ref[0, r, :], wp_ref[...]) for r in rows]
    x1s = [_layer_norm(alpha * x_ref[0, r, :] + mod[2:3] * y, ln[0:1], ln[1:2]) for r, y in zip(rows, ys)]
    hs = [(x1 * (1.0 + mod[4:5]) + mod[3:4]).astype(BF16) for x1 in x1s]
    gus = [(_dot(h, wg_ref[...]), _dot(h, wu_ref[...])) for h in hs]
    acts = [(_silu(g) * u).astype(BF16) for g, u in gus]
    y2s = [_dot(a, wd_ref[...]) for a in acts]
    for r, x1, y2 in zip(rows, x1s, y2s):
        o_ref[0, r, :] = _layer_norm(alpha * x1 + mod[5:6] * y2, ln[2:3], ln[3:4])


def _tail(x, z, mod, w_proj, w_gate, w_up, w_down, ln, *, alpha, tm):
    bsz, s, d = x.shape
    return pl.pallas_call(
        functools.partial(_tail_kernel, alpha=alpha, tm=tm),
        grid=(bsz, s // tm),
        in_specs=[
            pl.BlockSpec((1, tm, d), lambda b, t: (b, t, 0)),
            pl.BlockSpec((1, tm, d), lambda b, t: (b, t, 0)),
            pl.BlockSpec((1, N_MOD, d), lambda b, t: (b, 0, 0)),
            _const_spec(w_proj.shape),
            _const_spec(w_gate.shape),
            _const_spec(w_up.shape),
            _const_spec(w_down.shape),
            _const_spec(ln.shape),
        ],
        out_specs=pl.BlockSpec((1, tm, d), lambda b, t: (b, t, 0)),
        out_shape=jax.ShapeDtypeStruct((bsz, s, d), F32),
        compiler_params=_params(("parallel", "parallel"), "tail"),
        name="tail",
    )(x, z, mod, w_proj, w_gate, w_up, w_down, ln)


def _qkv_kernel(x_ref, mod_ref, modkv_ref, wk_ref, wvt_ref, wqt_ref, wf_ref, bf_ref,
                qt_ref, k_ref, vt_ref, aug_ref, carry_ref, *, tm, tk, n_heads, q_scale):
    @pl.when(pl.program_id(1) == 0)
    def _():
        carry_ref[...] = jnp.zeros_like(carry_ref)

    x = x_ref[0]
    mod = mod_ref[0]
    modkv = modkv_ref[0]
    hkv = (x * (1.0 + modkv[1:2]) + modkv[0:1]).astype(BF16)
    hq = (x * (1.0 + mod[1:2]) + mod[0:1]).astype(BF16)
    zf = _dot(hkv, wf_ref[...]) + bf_ref[...]
    k_ref[0] = _dot(hkv, wk_ref[...]).astype(BF16)
    vt = _dot_nt(wvt_ref[...], hkv).astype(BF16)
    for c in range(tm // tk):
        vt_ref[0, c] = vt[:, c * tk:(c + 1) * tk]
    qt_ref[0] = (_dot_nt(wqt_ref[...], hq) * q_scale).astype(BF16)

    lf = jnp.minimum(zf, 0.0) - jnp.log1p(jnp.exp(-jnp.abs(zf)))
    lane = lax.broadcasted_iota(jnp.int32, lf.shape, 1)
    lf = jnp.where(lane < n_heads, lf, 0.0)
    tri = jnp.where(lax.broadcasted_iota(jnp.int32, (ROW_CHUNK, ROW_CHUNK), 0)
                    >= lax.broadcasted_iota(jnp.int32, (ROW_CHUNK, ROW_CHUNK), 1), 1.0, 0.0).astype(BF16)
    total = carry_ref[0:1, :]
    for r in range(0, tm, ROW_CHUNK):
        hi, mid, lo = _split3(lf[r:r + ROW_CHUNK])
        fcum = _dot(tri, hi) + _dot(tri, mid) + _dot(tri, lo) + total
        total = fcum[ROW_CHUNK - 1:ROW_CHUNK, :]
        a_hi, a_mid, a_lo = _split3(-LOG2_E * fcum)
        aug = (a_hi.astype(F32) + pltpu.roll(a_mid.astype(F32), n_heads, axis=1)
               + pltpu.roll(a_lo.astype(F32), 2 * n_heads, axis=1))
        aug_ref[0, r:r + ROW_CHUNK, :] = aug.astype(BF16)
    carry_ref[0:1, :] = total


def _qkv(x, mod, modkv, w_k, w_vt, w_qt, w_f, b_f, *, tm, tk, n_heads):
    bsz, s, d = x.shape
    nt = s // tm
    lanes = w_f.shape[1]
    return pl.pallas_call(
        functools.partial(_qkv_kernel, tm=tm, tk=tk, n_heads=n_heads, q_scale=LOG2_E * HEAD_DIM ** -0.5),
        grid=(bsz, nt),
        in_specs=[
            pl.BlockSpec((1, tm, d), lambda b, t: (b, t, 0)),
            pl.BlockSpec((1, N_MOD, d), lambda b, t: (b, 0, 0)),
            pl.BlockSpec((1, 2, d), lambda b, t: (b, 0, 0)),
            _const_spec(w_k.shape),
            _const_spec(w_vt.shape),
            _const_spec(w_qt.shape),
            _const_spec(w_f.shape),
            _const_spec(b_f.shape),
        ],
        out_specs=[
            pl.BlockSpec((1, d, tm), lambda b, t: (b, 0, t)),
            pl.BlockSpec((1, tm, d), lambda b, t: (b, t, 0)),
            pl.BlockSpec((1, tm // tk, d, tk), lambda b, t: (b, t, 0, 0)),
            pl.BlockSpec((1, tm, lanes), lambda b, t: (b, t, 0)),
        ],
        out_shape=[
            jax.ShapeDtypeStruct((bsz, d, s), BF16),
            jax.ShapeDtypeStruct((bsz, s, d), BF16),
            jax.ShapeDtypeStruct((bsz, s // tk, d, tk), BF16),
            jax.ShapeDtypeStruct((bsz, s, lanes), BF16),
        ],
        scratch_shapes=[pltpu.VMEM((8, lanes), F32)],
        compiler_params=_params(("parallel", "arbitrary"), "qkv"),
        name="qkv",
    )(x, mod, modkv, w_k, w_vt, w_qt, w_f, b_f)


def _attn_kernel(qt_ref, k_ref, aug_ref, vt_ref, vtd_ref, o_ref, rhs_ref, s_ref, mt_ref, m_ref, acc_ref,
                 *, tq, tk, n_heads, n_streams, n_q):
    g = pl.program_id(1)
    i = pl.program_id(2)
    hd = HEAD_DIM
    blk = HEADS_PER_STEP * hd
    diag = 2
    ones = jnp.ones((SUM_ROWS, tk), BF16)

    def build_rhs():
        r = lax.broadcasted_iota(jnp.int32, (blk, tq), 0)
        for u in range(n_streams):
            qt = qt_ref[0, u * blk:(u + 1) * blk, :].astype(F32)
            for hh in range(HEADS_PER_STEP):
                head = (g * n_streams + u) * HEADS_PER_STEP + hh
                sel = (r == head) | (r == head + n_heads) | (r == head + 2 * n_heads)
                rhs_ref[u, 0:blk, hh * tq:(hh + 1) * tq] = jnp.where(
                    (r >= hh * hd) & (r < (hh + 1) * hd), qt, 0.0).astype(BF16)
                rhs_ref[u, blk:2 * blk, hh * tq:(hh + 1) * tq] = jnp.where(sel, 1.0, 0.0).astype(BF16)

    def reset_state():
        m_ref[...] = jnp.full_like(m_ref, NEG_BIG)
        acc_ref[...] = jnp.zeros_like(acc_ref)

    def finish():
        outs = []
        for u in range(n_streams):
            for hh in range(HEADS_PER_STEP):
                acc = acc_ref[u, hh]
                outs.append(acc[0:hd] / acc[hd:hd + 1])
        o_ref[0] = jnp.concatenate(outs, axis=0).T.astype(BF16)

    def qk_stage(u, j, buf, masked=False):
        ks = pl.multiple_of(j * tk, tk)
        kx = jnp.concatenate([k_ref[0, pl.ds(ks, tk), u * blk:(u + 1) * blk], aug_ref[0, pl.ds(ks, tk), :]], axis=1)
        s = _dot(kx, rhs_ref[u])
        if masked:
            krow = lax.broadcasted_iota(jnp.int32, s.shape, 0)
            qcol = lax.broadcasted_iota(jnp.int32, s.shape, 1)
            qcol = jnp.where(qcol >= tq, qcol - tq, qcol)
            s = jnp.where(krow <= qcol, s, NEG_BIG)
        s_ref[buf, u] = s
        mt_ref[buf, u] = jnp.max(s, axis=0, keepdims=True)

    def pv_stage(u, j, buf, pending=False):
        m_old = m_ref[u]
        m_new = jnp.maximum(m_old, mt_ref[buf, u])
        alpha = jnp.exp2(m_old - m_new)
        pr = jnp.exp2(s_ref[buf, u] - m_new).astype(BF16)
        m_ref[u] = m_new
        for hh in range(HEADS_PER_STEP):
            row0 = u * blk + hh * hd
            vt_h = vtd_ref[0, 0, row0:row0 + hd, :] if pending else vt_ref[0, j, row0:row0 + hd, :]
            vx = jnp.concatenate([vt_h, ones], axis=0)
            pv = _dot(vx, pr[:, hh * tq:(hh + 1) * tq])
            acc_ref[u, hh] = alpha[:, hh * tq:(hh + 1) * tq] * acc_ref[u, hh] + pv

    def all_streams(fn, *args, **kwargs):
        for u in range(n_streams):
            fn(u, *args, **kwargs)

    def step(t_next, buf_next, t_cur, buf_cur, masked=False, pending=False):
        for u in range(n_streams):
            qk_stage(u, t_next, buf_next, masked=masked)
            pv_stage(u, t_cur, buf_cur, pending=pending)

    inner = (i > 0) & (i < n_q)
    odd = i % 2 == 1

    @pl.when(i == 0)
    def _():
        build_rhs()
        reset_state()
        all_streams(qk_stage, 0, diag, masked=True)

    def start(buf_first):
        build_rhs()
        step(0, buf_first, i - 1, diag, pending=True)
        finish()
        reset_state()

    @pl.when(inner & odd)
    def _():
        start(1)

    @pl.when(inner & jnp.logical_not(odd))
    def _():
        start(0)
        step(1, 1, 0, 0)

    t_start = 1 - i % 2

    def two_steps(t):
        step(t + 1, 0, t, 1)
        step(t + 2, 1, t + 1, 0)

    def body(jj, carry):
        t = t_start + 4 * jj
        two_steps(t)
        two_steps(t + 2)
        return carry

    n_pairs = jnp.where(inner, (i - 1) // 2, 0)
    lax.fori_loop(0, n_pairs // 2, body, 0)

    @pl.when(n_pairs % 2 == 1)
    def _():
        two_steps(t_start + 2 * (n_pairs - 1))

    @pl.when(inner)
    def _():
        step(i, diag, i - 1, 1, masked=True)

    @pl.when(i == n_q)
    def _():
        all_streams(pv_stage, n_q - 1, diag, pending=True)
        finish()


def _attention(qt, k, aug, vt, *, tq, n_heads, n_streams):
    bsz, d, s = qt.shape
    nk, tk = vt.shape[1], vt.shape[3]
    assert tq == tk, "the causal mask is applied to the diagonal tile only"
    blk = HEADS_PER_STEP * HEAD_DIM
    wide = n_streams * blk
    lanes = aug.shape[2]
    n_q = s // tq
    n_g = d // wide

    def resident(b, g, i):
        last = i == n_q
        wrap = g + 1 == n_g
        b_next = jnp.where(wrap, jnp.minimum(b + 1, bsz - 1), b)
        g_next = jnp.where(wrap, 0, g + 1)
        return jnp.where(last, b_next, b), jnp.where(last, g_next, g)

    def k_map(b, g, i):
        bb, gg = resident(b, g, i)
        return bb, 0, gg

    def aug_map(b, g, i):
        return resident(b, g, i)[0], 0, 0

    def vt_map(b, g, i):
        bb, gg = resident(b, g, i)
        return bb, 0, gg, 0

    return pl.pallas_call(
        functools.partial(_attn_kernel, tq=tq, tk=tk, n_heads=n_heads, n_streams=n_streams, n_q=n_q),
        grid=(bsz, d // wide, n_q + 1),
        in_specs=[
            pl.BlockSpec((1, wide, tq), lambda b, g, i: (b, g, jnp.minimum(i, n_q - 1))),
            pl.BlockSpec((1, s, wide), k_map),
            pl.BlockSpec((1, s, lanes), aug_map),
            pl.BlockSpec((1, nk, wide, tk), vt_map),
            pl.BlockSpec((1, 1, wide, tk), lambda b, g, i: (b, jnp.maximum(i - 1, 0), g, 0)),
        ],
        out_specs=pl.BlockSpec((1, tq, wide), lambda b, g, i: (b, jnp.maximum(i - 1, 0), g)),
        out_shape=jax.ShapeDtypeStruct((bsz, s, d), BF16),
        scratch_shapes=[
            pltpu.VMEM((n_streams, blk + lanes, HEADS_PER_STEP * tq), BF16),
            pltpu.VMEM((3, n_streams, tk, HEADS_PER_STEP * tq), F32),
            pltpu.VMEM((3, n_streams, 1, HEADS_PER_STEP * tq), F32),
            pltpu.VMEM((n_streams, 1, HEADS_PER_STEP * tq), F32),
            pltpu.VMEM((n_streams, HEADS_PER_STEP, HEAD_DIM + SUM_ROWS, tq), F32),
        ],
        compiler_params=_params(("parallel", "parallel", "arbitrary"), "attention"),
        name="attention",
    )(qt, k, aug, vt, vt)


def kernel(x, c, w_ada, b_ada, conv_w_in, conv_w, conv_w_out, w_ada_kv, b_ada_kv, w_k, w_v, w_f, b_f,
           attn_w_q, attn_w_o, ffn_w_gate, ffn_w_up, ffn_w_down, ln1_g, ln1_b, ln2_g, ln2_b):
    depth = w_ada.shape[0]
    n_a = conv_w_in.shape[0]
    d = x.shape[-1]
    n_heads = w_f.shape[1]
    assert d // n_heads == HEAD_DIM and 3 * n_heads <= V7X_LANES
    alpha = (2.0 * depth) ** 0.25
    tm = ROW_CHUNK
    tm_tail = 2 * ROW_CHUNK
    tm_proj = 4 * ROW_CHUNK

    mods, modkv = _mods(c, w_ada, b_ada, w_ada_kv, b_ada_kv)
    bf = lambda w: w.astype(BF16)
    w_f_pad = jnp.pad(w_f, ((0, 0), (0, V7X_LANES - n_heads))).astype(BF16)
    b_f_pad = jnp.pad(b_f, (0, V7X_LANES - n_heads)).reshape(1, V7X_LANES)

    kv = None
    for l in range(depth):
        ln = jnp.stack([ln1_g[l], ln1_b[l], ln2_g[l], ln2_b[l]])
        if l < n_a:
            z = _conv_front(x, mods[l], bf(conv_w_in[l]), conv_w[l], tm=tm_proj)
            w_proj = conv_w_out[l]
        else:
            j = l - n_a
            if kv is None:
                qt, k, vt, aug = _qkv(x, mods[l], modkv, bf(w_k), bf(w_v.T), bf(attn_w_q[j].T), w_f_pad, b_f_pad,
                                      tm=tm_proj, tk=tm, n_heads=n_heads)
                kv = (k, aug, vt)
            else:
                raise NotImplementedError("more than one attention layer needs a q-only projection kernel")
            z = _attention(qt, *kv, tq=tm, n_heads=n_heads, n_streams=8)
            w_proj = attn_w_o[j]
        x = _tail(x, z, mods[l], bf(w_proj), bf(ffn_w_gate[l]), bf(ffn_w_up[l]), bf(ffn_w_down[l]), ln,
                  alpha=alpha, tm=tm_tail)
    return x
```
